```python
import math
import jax, jax.numpy as jnp
from jax import lax
import numpy as np

D_MODEL = 4096
BATCH = 1
SEQ = 16384
DEPTH = 4

HEAD_DIM = 128
MIX_UNIT = D_MODEL // 4
DA_DH = HEAD_DIM
DA_DV = 2 * DA_DH
DA_HEADS = MIX_UNIT // DA_DV
GQA_DH = HEAD_DIM
GQA_HEADS = MIX_UNIT // GQA_DH
GQA_KV_HEADS = GQA_HEADS // 4
RET_DK = HEAD_DIM
RET_DV = 2 * RET_DK
RET_HEADS = MIX_UNIT // RET_DV
RET_CHUNK = 128
HY_WIDTH = MIX_UNIT
HY_ORDER = 2
HY_EMB = 33
HY_BANDS = (HY_EMB - 1) // 2
HY_FILTER_HIDDEN = 64
HY_DECAY_TARGET = 1e-2
HY_FAST_PCT = 0.3
HY_SLOW_PCT = 1.5
SHORT_CONV = 3
GATE_RANK = 512
N_BRANCH = 4
XA_HEADS = 4
XA_DH = HEAD_DIM
N_MEM = 256
D_FF = D_MODEL
GRID_W = 64
Q_BLOCK = 128
ROPE_THETA = 10000.0
EPS = 1e-6

DA_QK_W = DA_HEADS * 2 * DA_DH
DA_V_W = DA_HEADS * DA_DV
GQA_Q_W = GQA_HEADS * GQA_DH
GQA_KV_W = GQA_KV_HEADS * GQA_DH
RET_QK_W = RET_HEADS * RET_DK
RET_V_W = RET_HEADS * RET_DV
HY_PROJ_W = (HY_ORDER + 1) * HY_WIDTH
IN_WIDTHS = (DA_QK_W, DA_QK_W, DA_V_W, GQA_Q_W, GQA_KV_W, GQA_KV_W,
             RET_QK_W, RET_QK_W, RET_V_W, RET_V_W, HY_PROJ_W, GATE_RANK)
IN_COLS = sum(IN_WIDTHS)

kernel_name = 'hybrid_gated_bidir_encoder'


def rms_norm(x, gain):
    xf = x.astype(jnp.float32)
    y = xf * lax.rsqrt(jnp.mean(xf * xf, axis=-1, keepdims=True) + EPS)
    return (y * gain.astype(jnp.float32)).astype(x.dtype)


def swiglu(h, w1, w3, w2):
    return (jax.nn.silu(h @ w1) * (h @ w3)) @ w2


def split_columns(a, widths):
    idx = np.cumsum(widths)[:-1].tolist()
    return jnp.split(a, idx, axis=-1)


def alibi_slopes(n_heads):
    return 2.0 ** (-8.0 * jnp.arange(1, n_heads + 1, dtype=jnp.float32) / n_heads)


def axial_rope_tables(S):
    n_rows = S // GRID_W
    row = jnp.broadcast_to(jnp.arange(n_rows, dtype=jnp.float32)[:, None], (n_rows, GRID_W)).reshape(S)
    col = jnp.broadcast_to(jnp.arange(GRID_W, dtype=jnp.float32)[None, :], (n_rows, GRID_W)).reshape(S)
    axis_dim = GQA_DH // 2
    inv_freq = ROPE_THETA ** (-jnp.arange(0, axis_dim, 2, dtype=jnp.float32) / axis_dim)
    ang = jnp.stack([row[:, None] * inv_freq, col[:, None] * inv_freq], axis=1)
    ang = jnp.broadcast_to(ang[:, :, None, :], (S, 2, 2, axis_dim // 2)).reshape(S, GQA_DH)
    return jnp.cos(ang), jnp.sin(ang)


def apply_axial_rope(x, cos, sin):
    xf = x.astype(jnp.float32)
    u = xf.reshape(*xf.shape[:-1], 2, 2, GQA_DH // 4)
    rot = jnp.stack([-u[..., 1, :], u[..., 0, :]], axis=-2).reshape(xf.shape)
    return (xf * cos[None, :, None] + rot * sin[None, :, None]).astype(x.dtype)


def diff_attention(q, k, v, lam, slopes):
    B, S, H, _, dh = q.shape
    nb = S // Q_BLOCK
    scale = dh ** -0.5
    kpos = jnp.arange(S, dtype=jnp.float32)
    qb = jnp.moveaxis(q.reshape(B, nb, Q_BLOCK, H, 2, dh), 1, 0)
    qpos = kpos.reshape(nb, Q_BLOCK)

    def one_block(args):
        qi, pi = args
        s = jnp.einsum('bqhcd,bkhcd->bhcqk', qi, k, preferred_element_type=jnp.float32) * scale
        dist = jnp.abs(pi[:, None] - kpos[None, :])
        s = s - slopes[None, :, None, None, None] * dist[None, None, None]
        p = jax.nn.softmax(s, axis=-1)
        a = p[:, :, 0] - lam * p[:, :, 1]
        return jnp.einsum('bhqk,bkhd->bqhd', a.astype(v.dtype), v)

    o = lax.map(one_block, (qb, qpos))
    return jnp.moveaxis(o, 0, 1).reshape(B, S, H, v.shape[-1])


def gqa_attention(q, k, v):
    B, S, Hq, dh = q.shape
    Hkv = k.shape[2]
    nb = S // Q_BLOCK
    scale = dh ** -0.5
    qb = jnp.moveaxis(q.reshape(B, nb, Q_BLOCK, Hkv, Hq // Hkv, dh), 1, 0)

    def one_block(qi):
        s = jnp.einsum('bqgrd,bkgd->bgrqk', qi, k, preferred_element_type=jnp.float32) * scale
        p = jax.nn.softmax(s, axis=-1)
        return jnp.einsum('bgrqk,bkgd->bqgrd', p.astype(v.dtype), v)

    o = lax.map(one_block, qb)
    return jnp.moveaxis(o, 0, 1).reshape(B, S, Hq * dh)


def retention_scan(q, k, v, log_g, inclusive):
    B, S, H, dk = q.shape
    dv = v.shape[-1]
    C = RET_CHUNK
    n = S // C
    idx = jnp.arange(C, dtype=jnp.float32)
    diff = idx[:, None] - idx[None, :]
    mask = (diff >= 0) if inclusive else (diff > 0)
    dmask = jnp.where(mask[None], jnp.exp(jnp.where(mask, diff, 0.0)[None] * log_g[:, None, None]), 0.0)
    qc = q.reshape(B, n, C, H, dk)
    kc = k.reshape(B, n, C, H, dk)
    vc = v.reshape(B, n, C, H, dv)
    s = jnp.einsum('bnqhd,bnkhd->bnhqk', qc, kc) * dmask[None, None]
    intra = jnp.einsum('bnhqk,bnkhe->bnqhe', s, vc)
    zeta = jnp.exp((C - 1 - idx)[:, None] * log_g[None])
    kv = jnp.einsum('bnkhd,bnkhe->nbhde', kc * zeta[None, None, :, :, None], vc)
    g_chunk = jnp.exp(C * log_g)[None, :, None, None]

    def step(state, kv_n):
        return g_chunk * state + kv_n, state

    _, prev = lax.scan(step, jnp.zeros((B, H, dk, dv), jnp.float32), kv)
    xi = jnp.exp((idx + 1.0)[:, None] * log_g[None])
    cross = jnp.einsum('bnqhd,nbhde->bnqhe', qc * xi[None, None, :, :, None], prev)
    return (intra + cross).reshape(B, S, H, dv)


def short_conv_centred(u, w, b):
    up = jnp.pad(u, ((0, 0), (1, 1), (0, 0)))
    return up[:, :-2] * w[0] + up[:, 1:-1] * w[1] + up[:, 2:] * w[2] + b


def hyena_position_tables(L):
    pos = jnp.arange(L, dtype=jnp.float32)
    t = pos / (L - 1)
    w = 2.0 * math.pi * pos / L
    f = jnp.linspace(1e-4, HY_BANDS - 1, HY_BANDS, dtype=jnp.float32)
    ang = w[:, None] * f[None]
    feats = jnp.concatenate([t[:, None], jnp.cos(ang), -jnp.sin(ang)], axis=-1)
    deltas = jnp.abs(jnp.linspace(math.log(HY_DECAY_TARGET) / HY_SLOW_PCT,
                                  math.log(HY_DECAY_TARGET) / HY_FAST_PCT, HY_WIDTH, dtype=jnp.float32))
    window = jnp.exp(-t[:, None] * deltas[None])
    return feats, window


def hyena_filters(feats, window, w1, b1, w2, b2, w3, freq):
    f32 = jnp.float32
    L = feats.shape[0]
    fr = freq.astype(f32)
    hid = jnp.sin(fr * (feats @ w1.astype(f32) + b1.astype(f32)))
    hid = jnp.sin(fr * (hid @ w2.astype(f32) + b2.astype(f32)))
    h = (hid @ w3.astype(f32)).reshape(L, 2, HY_ORDER, HY_WIDTH) * window[:, None, None, :]
    l1 = jnp.sum(jnp.abs(h[:, 0]), axis=0) + jnp.sum(jnp.abs(h[1:, 1]), axis=0)
    return h / (l1 + EPS)


def bidir_long_conv(z, h_fwd, h_bwd, d_term):
    L = z.shape[1]
    k_full = jnp.concatenate([h_fwd, jnp.zeros_like(h_fwd[:1]), h_bwd[:0:-1]], axis=0)
    k_f = jnp.fft.rfft(k_full, axis=0)
    z_f = jnp.fft.rfft(z, n=2 * L, axis=1)
    y = jnp.fft.irfft(z_f * k_f[None], n=2 * L, axis=1)[:, :L]
    return y + z * d_term


def hyena_mixer(u, conv_w, conv_b, filt, d_term):
    u = short_conv_centred(u, conv_w, conv_b)
    parts = jnp.split(u, HY_ORDER + 1, axis=-1)
    z = parts[-1]
    for n in range(HY_ORDER):
        z = parts[n] * bidir_long_conv(z, filt[:, 0, n], filt[:, 1, n], d_term[n])
    return z


def hybrid_mixer(h, layer, w_in, diff_lambda, diff_norm, gqa_q_norm, gqa_k_norm, ret_decay_logit,
                 ret_norm, hy_conv_w, hy_conv_b, hy_filt, hy_filter_bias, w_gate_up, b_gate,
                 w_branch, w_out, slopes, rope_cos, rope_sin):
    f32 = jnp.float32
    B, S, _ = h.shape
    (qa, ka, va, qb, kb, vb, qc, kc, vc, gc, hy_u, gate_low) = split_columns(h @ w_in, IN_WIDTHS)

    lam_init = 0.8 - 0.6 * math.exp(-0.3 * layer)
    lp = diff_lambda.astype(f32)
    lam = jnp.exp(jnp.sum(lp[0] * lp[1])) - jnp.exp(jnp.sum(lp[2] * lp[3])) + lam_init
    oa = diff_attention(qa.reshape(B, S, DA_HEADS, 2, DA_DH), ka.reshape(B, S, DA_HEADS, 2, DA_DH),
                        va.reshape(B, S, DA_HEADS, DA_DV), lam, slopes)
    oa = (rms_norm(oa, diff_norm) * (1.0 - lam_init)).reshape(B, S, MIX_UNIT)

    qb = apply_axial_rope(rms_norm(qb.reshape(B, S, GQA_HEADS, GQA_DH), gqa_q_norm), rope_cos, rope_sin)
    kb = apply_axial_rope(rms_norm(kb.reshape(B, S, GQA_KV_HEADS, GQA_DH), gqa_k_norm), rope_cos, rope_sin)
    ob = gqa_attention(qb, kb, vb.reshape(B, S, GQA_KV_HEADS, GQA_DH))

    log_g = -jax.nn.softplus(-ret_decay_logit.astype(f32))
    qr = qc.reshape(B, S, RET_HEADS, RET_DK).astype(f32)
    kr = kc.reshape(B, S, RET_HEADS, RET_DK).astype(f32) * (RET_DK ** -0.5)
    vr = vc.reshape(B, S, RET_HEADS, RET_DV).astype(f32)
    y_fwd = retention_scan(qr, kr, vr, log_g[0], True)
    y_bwd = jnp.flip(retention_scan(jnp.flip(qr, 1), jnp.flip(kr, 1), jnp.flip(vr, 1), log_g[1], False), 1)
    oc = rms_norm(y_fwd + y_bwd, ret_norm) * jax.nn.silu(gc.astype(f32).reshape(B, S, RET_HEADS, RET_DV))
    oc = oc.reshape(B, S, MIX_UNIT).astype(h.dtype)

    od = hyena_mixer(hy_u.astype(f32), hy_conv_w.astype(f32), hy_conv_b.astype(f32), hy_filt,
                     hy_filter_bias.astype(f32)).astype(h.dtype)

    merged = None
    for i, o in enumerate((oa, ob, oc, od)):
        gate = jax.nn.sigmoid(gate_low @ w_gate_up[i] + b_gate[i])
        term = gate * (o @ w_branch[i])
        merged = term if merged is None else merged + term
    return merged @ w_out


def memory_cross_attention(h, mem, mem_norm, wq, wkv, wo):
    B, S, _ = h.shape
    q = (h @ wq).reshape(B, S, XA_HEADS, XA_DH)
    kv = (rms_norm(mem, mem_norm) @ wkv).reshape(B, mem.shape[1], 2, XA_HEADS, XA_DH)
    s = jnp.einsum('bqhd,bkhd->bhqk', q, kv[:, :, 0], preferred_element_type=jnp.float32) * (XA_DH ** -0.5)
    p = jax.nn.softmax(s, axis=-1)
    o = jnp.einsum('bhqk,bkhd->bqhd', p.astype(h.dtype), kv[:, :, 1]).reshape(B, S, XA_HEADS * XA_DH)
    return o @ wo


def setup_inputs(seed: int = 0) -> dict:
    key = jax.random.key(seed)
    keys = list(jax.random.split(key, 48))
    f32 = jnp.float32

    def nrm(shape, scale):
        return jax.random.normal(keys.pop(), shape, f32) * scale

    def gain(width):
        return 1.0 + nrm((DEPTH, width), 0.05)

    ret_base = jnp.asarray(np.log(2.0 ** (5 + np.arange(RET_HEADS)) - 1.0), dtype=f32)
    inputs = {}
    inputs['x'] = nrm((BATCH, SEQ, D_MODEL), 1.0)
    inputs['mem'] = nrm((BATCH, N_MEM, D_MODEL), 1.0)
    inputs['ffn1_pre_norm'] = gain(D_MODEL)
    inputs['ffn1_w1'] = nrm((DEPTH, D_MODEL, D_FF), D_MODEL ** -0.5)
    inputs['ffn1_w3'] = nrm((DEPTH, D_MODEL, D_FF), D_MODEL ** -0.5)
    inputs['ffn1_w2'] = nrm((DEPTH, D_FF, D_MODEL), D_FF ** -0.5)
    inputs['ffn1_post_norm'] = gain(D_MODEL)
    inputs['mix_pre_norm'] = gain(D_MODEL)
    inputs['w_in'] = nrm((DEPTH, D_MODEL, IN_COLS), D_MODEL ** -0.5)
    inputs['diff_lambda'] = nrm((DEPTH, 4, DA_DH), 0.1)
    inputs['diff_norm'] = gain(DA_DV)
    inputs['gqa_q_norm'] = gain(GQA_DH)
    inputs['gqa_k_norm'] = gain(GQA_DH)
    inputs['ret_decay_logit'] = ret_base[None, None, :] + nrm((DEPTH, 2, RET_HEADS), 0.05)
    inputs['ret_norm'] = gain(RET_DV)
    inputs['hy_conv_w'] = nrm((DEPTH, SHORT_CONV, HY_PROJ_W), SHORT_CONV ** -0.5)
    inputs['hy_conv_b'] = nrm((DEPTH, HY_PROJ_W), 0.02)
    inputs['hy_w1'] = nrm((DEPTH, HY_EMB, HY_FILTER_HIDDEN), HY_EMB ** -0.5)
    inputs['hy_b1'] = nrm((DEPTH, HY_FILTER_HIDDEN), 0.1)
    inputs['hy_w2'] = nrm((DEPTH, HY_FILTER_HIDDEN, HY_FILTER_HIDDEN), HY_FILTER_HIDDEN ** -0.5)
    inputs['hy_b2'] = nrm((DEPTH, HY_FILTER_HIDDEN), 0.1)
    inputs['hy_w3'] = nrm((DEPTH, HY_FILTER_HIDDEN, 2 * HY_ORDER * HY_WIDTH), HY_FILTER_HIDDEN ** -0.5)
    inputs['hy_sin_freq'] = 1.0 + nrm((DEPTH, HY_FILTER_HIDDEN), 0.05)
    inputs['hy_filter_bias'] = nrm((DEPTH, HY_ORDER, HY_WIDTH), 0.5)
    inputs['w_gate_up'] = nrm((DEPTH, N_BRANCH, GATE_RANK, D_MODEL), GATE_RANK ** -0.5)
    inputs['b_gate'] = nrm((DEPTH, N_BRANCH, D_MODEL), 0.02)
    inputs['w_branch'] = nrm((DEPTH, N_BRANCH, MIX_UNIT, D_MODEL), MIX_UNIT ** -0.5)
    inputs['w_out'] = nrm((DEPTH, D_MODEL, D_MODEL), D_MODEL ** -0.5)
    inputs['mix_post_norm'] = gain(D_MODEL)
    inputs['xa_pre_norm'] = gain(D_MODEL)
    inputs['xa_mem_norm'] = gain(D_MODEL)
    inputs['xa_wq'] = nrm((DEPTH, D_MODEL, XA_HEADS * XA_DH), D_MODEL ** -0.5)
    inputs['xa_wkv'] = nrm((DEPTH, D_MODEL, 2 * XA_HEADS * XA_DH), D_MODEL ** -0.5)
    inputs['xa_wo'] = nrm((DEPTH, XA_HEADS * XA_DH, D_MODEL), (XA_HEADS * XA_DH) ** -0.5)
    inputs['xa_post_norm'] = gain(D_MODEL)
    inputs['ffn2_pre_norm'] = gain(D_MODEL)
    inputs['ffn2_w1'] = nrm((DEPTH, D_MODEL, D_FF), D_MODEL ** -0.5)
    inputs['ffn2_w3'] = nrm((DEPTH, D_MODEL, D_FF), D_MODEL ** -0.5)
    inputs['ffn2_w2'] = nrm((DEPTH, D_FF, D_MODEL), D_FF ** -0.5)
    inputs['ffn2_post_norm'] = gain(D_MODEL)
    return inputs


def reference(x, mem, ffn1_pre_norm, ffn1_w1, ffn1_w3, ffn1_w2, ffn1_post_norm,
              mix_pre_norm, w_in, diff_lambda, diff_norm, gqa_q_norm, gqa_k_norm, ret_decay_logit,
              ret_norm, hy_conv_w, hy_conv_b, hy_w1, hy_b1, hy_w2, hy_b2, hy_w3, hy_sin_freq,
              hy_filter_bias, w_gate_up, b_gate, w_branch, w_out, mix_post_norm,
              xa_pre_norm, xa_mem_norm, xa_wq, xa_wkv, xa_wo, xa_post_norm,
              ffn2_pre_norm, ffn2_w1, ffn2_w3, ffn2_w2, ffn2_post_norm):
    S = x.shape[1]
    slopes = alibi_slopes(DA_HEADS)
    rope_cos, rope_sin = axial_rope_tables(S)
    hy_feats, hy_window = hyena_position_tables(S)
    for l in range(DEPTH):
        h = rms_norm(x, ffn1_pre_norm[l])
        x = x + 0.5 * rms_norm(swiglu(h, ffn1_w1[l], ffn1_w3[l], ffn1_w2[l]), ffn1_post_norm[l])

        hy_filt = hyena_filters(hy_feats, hy_window, hy_w1[l], hy_b1[l], hy_w2[l], hy_b2[l], hy_w3[l],
                                hy_sin_freq[l])
        h = rms_norm(x, mix_pre_norm[l])
        y = hybrid_mixer(h, l, w_in[l], diff_lambda[l], diff_norm[l], gqa_q_norm[l], gqa_k_norm[l],
                         ret_decay_logit[l], ret_norm[l], hy_conv_w[l], hy_conv_b[l], hy_filt,
                         hy_filter_bias[l], w_gate_up[l], b_gate[l], w_branch[l], w_out[l],
                         slopes, rope_cos, rope_sin)
        x = x + rms_norm(y, mix_post_norm[l])

        h = rms_norm(x, xa_pre_norm[l])
        y = memory_cross_attention(h, mem, xa_mem_norm[l], xa_wq[l], xa_wkv[l], xa_wo[l])
        x = x + rms_norm(y, xa_post_norm[l])

        h = rms_norm(x, ffn2_pre_norm[l])
        x = x + 0.5 * rms_norm(swiglu(h, ffn2_w1[l], ffn2_w3[l], ffn2_w2[l]), ffn2_post_norm[l])
    return x
```

```python
import functools
import math

import numpy as np
import jax
import jax.numpy as jnp
from jax import lax
from jax.experimental import pallas as pl
from jax.experimental.pallas import tpu as pltpu

F32 = jnp.float32
BF16 = jnp.bfloat16

EPS = 1e-6
HEAD_DIM = 128
GATE_RANK = 512
GRID_W = 64
ROPE_THETA = 10000.0
RET_CHUNK = 128
XA_HEADS = 4
HY_ORDER = 2
HY_EMB = 33
HY_DECAY_TARGET = 1e-2
HY_FAST_PCT = 0.3
HY_SLOW_PCT = 1.5
LOG2E = math.log2(math.e)
DFT_N2 = 256
NEG_BIG = -1e30

VMEM_LIMIT_BYTES = 56 * 1024 * 1024


def _cp(n_axes):
    return pltpu.CompilerParams(dimension_semantics=("arbitrary",) * n_axes,
                                vmem_limit_bytes=VMEM_LIMIT_BYTES)


def _tile(n, pref, align=128):
    for t in range(min(n, pref), 0, -1):
        if n % t == 0 and t % align == 0:
            return t
    raise ValueError((n, pref, align))


def _rms(x, gain):
    return x * lax.rsqrt(jnp.mean(x * x, axis=-1, keepdims=True) + EPS) * gain


def _norm_kernel(x_ref, g_ref, h_ref):
    h_ref[...] = _rms(x_ref[...], g_ref[...]).astype(h_ref.dtype)


def _norm_cast(x, gain):
    m, d = x.shape
    tm = _tile(m, 256)
    row = pl.BlockSpec((tm, d), lambda i: (i, 0))
    vec = pl.BlockSpec((1, d), lambda i: (0, 0))
    return pl.pallas_call(
        _norm_kernel, grid=(m // tm,), in_specs=[row, vec], out_specs=row,
        out_shape=jax.ShapeDtypeStruct((m, d), BF16), compiler_params=_cp(1),
        name="norm_cast")(x, gain.reshape(1, d))


def _resid_norm_kernel(x_ref, y_ref, gp_ref, gn_ref, xo_ref, ho_ref, *, alpha):
    xn = x_ref[...] + alpha * _rms(y_ref[...], gp_ref[...])
    xo_ref[...] = xn
    ho_ref[...] = _rms(xn, gn_ref[...]).astype(ho_ref.dtype)


def _resid_norm(x, y, g_post, alpha, g_next):
    m, d = x.shape
    tm = _tile(m, 256)
    row = pl.BlockSpec((tm, d), lambda i: (i, 0))
    vec = pl.BlockSpec((1, d), lambda i: (0, 0))
    return pl.pallas_call(
        functools.partial(_resid_norm_kernel, alpha=alpha), grid=(m // tm,),
        in_specs=[row, row, vec, vec], out_specs=[row, row],
        out_shape=[jax.ShapeDtypeStruct((m, d), F32), jax.ShapeDtypeStruct((m, d), BF16)],
        compiler_params=_cp(1), name="resid_norm")(x, y, g_post.reshape(1, d), g_next.reshape(1, d))


def _mm_kernel(a_ref, w_ref, s_ref, o_ref):
    acc = jnp.dot(a_ref[...], w_ref[...], preferred_element_type=F32)
    o_ref[...] = (acc * s_ref[...]).astype(o_ref.dtype)


def _matmul(a, w, out_dtype, colscale=None, tm=512, tn=1024):
    m, k = a.shape
    n = w.shape[1]
    tm, tn = _tile(m, tm), _tile(n, tn)
    if colscale is None:
        colscale = jnp.ones((n,), F32)
    return pl.pallas_call(
        _mm_kernel, grid=(n // tn, m // tm),
        in_specs=[pl.BlockSpec((tm, k), lambda j, i: (i, 0)),
                  pl.BlockSpec((k, tn), lambda j, i: (0, j)),
                  pl.BlockSpec((1, tn), lambda j, i: (0, j))],
        out_specs=pl.BlockSpec((tm, tn), lambda j, i: (i, j)),
        out_shape=jax.ShapeDtypeStruct((m, n), out_dtype),
        compiler_params=_cp(2), name="matmul")(a, w, colscale.reshape(1, n).astype(F32))


def _ffn_up_kernel(h_ref, w1_ref, w3_ref, o_ref):
    h = h_ref[...]
    a = jnp.dot(h, w1_ref[...], preferred_element_type=F32)
    b = jnp.dot(h, w3_ref[...], preferred_element_type=F32)
    o_ref[...] = (a * jax.nn.sigmoid(a) * b).astype(o_ref.dtype)


def _ffn_up(h, w1, w3, tm=512, tn=512):
    m, k = h.shape
    n = w1.shape[1]
    tm, tn = _tile(m, tm), _tile(n, tn)
    wspec = pl.BlockSpec((k, tn), lambda j, i: (0, j))
    return pl.pallas_call(
        _ffn_up_kernel, grid=(n // tn, m // tm),
        in_specs=[pl.BlockSpec((tm, k), lambda j, i: (i, 0)), wspec, wspec],
        out_specs=pl.BlockSpec((tm, tn), lambda j, i: (i, j)),
        out_shape=jax.ShapeDtypeStruct((m, n), BF16),
        compiler_params=_cp(2), name="ffn_up")(h, w1, w3)


def _ffn(x, h, w1, w3, w2, g_post, g_next):
    u = _ffn_up(h, w1.astype(BF16), w3.astype(BF16))
    y = _matmul(u, w2.astype(BF16), F32)
    return _resid_norm(x, y, g_post, 0.5, g_next)


def _flash_step(q, k, v, bias, m_ref, l_ref, acc_ref):
    s = lax.dot_general(q, k, (((1,), (1,)), ((), ())), preferred_element_type=F32)
    if bias is not None:
        s = s - bias
    m_prev = m_ref[...]
    m_new = jnp.maximum(m_prev, jnp.max(s, axis=-1, keepdims=True))
    alpha = jnp.exp2(m_prev - m_new)
    p = jnp.exp2(s - m_new)
    l_ref[...] = alpha * l_ref[...] + jnp.sum(p, axis=-1, keepdims=True)
    acc_ref[...] = alpha * acc_ref[...] + jnp.dot(p.astype(BF16), v, preferred_element_type=F32)
    m_ref[...] = m_new


def _diff_attn_kernel(scal_ref, q_ref, k_ref, v_ref, g_ref, o_ref, m_ref, l_ref, acc_ref,
                      *, tq, tk, dh, post_scale):
    h, qi, kj = pl.program_id(0), pl.program_id(1), pl.program_id(2)

    @pl.when(kj == 0)
    def _():
        m_ref[...] = jnp.full(m_ref.shape, NEG_BIG, F32)
        l_ref[...] = jnp.zeros(l_ref.shape, F32)
        acc_ref[...] = jnp.zeros(acc_ref.shape, F32)

    row = lax.broadcasted_iota(jnp.int32, (tq, tk), 0)
    col = lax.broadcasted_iota(jnp.int32, (tq, tk), 1)
    dist = jnp.abs(row - col + (qi * tq - kj * tk)).astype(F32)
    bias = dist * scal_ref[1 + h]
    k = k_ref[...]
    v = v_ref[...]
    for c in range(2):
        _flash_step(q_ref[:, c * dh:(c + 1) * dh], k[:, c * dh:(c + 1) * dh], v, bias,
                    m_ref.at[c], l_ref.at[c], acc_ref.at[c])

    @pl.when(kj == pl.num_programs(2) - 1)
    def _():
        o = acc_ref[0] / l_ref[0] - scal_ref[0] * (acc_ref[1] / l_ref[1])
        o_ref[...] = (_rms(o, g_ref[...]) * post_scale).astype(o_ref.dtype)


def _diff_attention(proj, scal, gain, *, seq, heads, q_blk, k_blk, v_blk, post_scale, tq=512, tk=1024):
    dh = HEAD_DIM
    dv = 2 * dh
    tq, tk = _tile(seq, tq), _tile(seq, tk)
    kern = functools.partial(_diff_attn_kernel, tq=tq, tk=tk, dh=dh, post_scale=post_scale)
    return pl.pallas_call(
        kern, grid=(heads, seq // tq, seq // tk),
        in_specs=[pl.BlockSpec(memory_space=pltpu.SMEM),
                  pl.BlockSpec((tq, 2 * dh), lambda h, i, j: (i, q_blk + h)),
                  pl.BlockSpec((tk, 2 * dh), lambda h, i, j: (j, k_blk + h)),
                  pl.BlockSpec((tk, dv), lambda h, i, j: (j, v_blk + h)),
                  pl.BlockSpec((1, dv), lambda h, i, j: (0, 0))],
        out_specs=pl.BlockSpec((tq, dv), lambda h, i, j: (i, h)),
        out_shape=jax.ShapeDtypeStruct((seq, heads * dv), BF16),
        scratch_shapes=[pltpu.VMEM((2, tq, 1), F32), pltpu.VMEM((2, tq, 1), F32),
                        pltpu.VMEM((2, tq, dv), F32)],
        compiler_params=_cp(3), name="diff_attention")(scal, proj, proj, proj, gain.reshape(1, dv))


def _gqa_prep_kernel(q_ref, k_ref, cos_ref, sin_ref, gq_ref, gk_ref, qo_ref, ko_ref, *, qscale):
    cos = cos_ref[...]
    sin = sin_ref[...]
    dh = cos.shape[-1]
    lane = lax.broadcasted_iota(jnp.int32, cos.shape, 1)
    first_half = (lane % (dh // 2)) < (dh // 4)

    def norm_rope(xh, gain):
        xn = _rms(xh.astype(F32), gain)
        rot = jnp.where(first_half, pltpu.roll(xn, dh - dh // 4, 1), pltpu.roll(xn, dh // 4, 1))
        return xn * cos + rot * sin

    for hh in range(q_ref.shape[1] // dh):
        sl = slice(hh * dh, (hh + 1) * dh)
        qo_ref[:, sl] = (norm_rope(q_ref[:, sl], gq_ref[...]) * qscale).astype(qo_ref.dtype)
    for hh in range(k_ref.shape[1] // dh):
        sl = slice(hh * dh, (hh + 1) * dh)
        ko_ref[:, sl] = norm_rope(k_ref[:, sl], gk_ref[...]).astype(ko_ref.dtype)


def _gqa_prep(proj, cos, sin_signed, gq, gk, *, seq, q_w, kv_w, q_blk, k_blk, qscale):
    dh = HEAD_DIM
    tm = _tile(seq, 512)
    vec = pl.BlockSpec((1, dh), lambda i: (0, 0))
    tab = pl.BlockSpec((tm, dh), lambda i: (i, 0))
    return pl.pallas_call(
        functools.partial(_gqa_prep_kernel, qscale=qscale), grid=(seq // tm,),
        in_specs=[pl.BlockSpec((tm, q_w), lambda i: (i, q_blk)),
                  pl.BlockSpec((tm, kv_w), lambda i: (i, k_blk)), tab, tab, vec, vec],
        out_specs=[pl.BlockSpec((tm, q_w), lambda i: (i, 0)), pl.BlockSpec((tm, kv_w), lambda i: (i, 0))],
        out_shape=[jax.ShapeDtypeStruct((seq, q_w), BF16), jax.ShapeDtypeStruct((seq, kv_w), BF16)],
        compiler_params=_cp(1), name="gqa_prep")(proj, proj, cos, sin_signed, gq.reshape(1, dh),
                                                gk.reshape(1, dh))


def _gqa_attn_kernel(q_ref, k_ref, v_ref, o_ref, m_ref, l_ref, acc_ref, *, rep, dh):
    kj = pl.program_id(2)

    @pl.when(kj == 0)
    def _():
        m_ref[...] = jnp.full(m_ref.shape, NEG_BIG, F32)
        l_ref[...] = jnp.zeros(l_ref.shape, F32)
        acc_ref[...] = jnp.zeros(acc_ref.shape, F32)

    k = k_ref[...]
    v = v_ref[...]
    for r in range(rep):
        _flash_step(q_ref[:, r * dh:(r + 1) * dh], k, v, None, m_ref.at[r], l_ref.at[r], acc_ref.at[r])

    @pl.when(kj == pl.num_programs(2) - 1)
    def _():
        for r in range(rep):
            o_ref[:, r * dh:(r + 1) * dh] = (acc_ref[r] / l_ref[r]).astype(o_ref.dtype)


def _gqa_attention(qg, kg, proj, *, seq, kv_heads, rep, v_blk, tq=256, tk=1024):
    dh = HEAD_DIM
    tq, tk = _tile(seq, tq), _tile(seq, tk)
    return pl.pallas_call(
        functools.partial(_gqa_attn_kernel, rep=rep, dh=dh), grid=(kv_heads, seq // tq, seq // tk),
        in_specs=[pl.BlockSpec((tq, rep * dh), lambda g, i, j: (i, g)),
                  pl.BlockSpec((tk, dh), lambda g, i, j: (j, g)),
                  pl.BlockSpec((tk, dh), lambda g, i, j: (j, v_blk + g))],
        out_specs=pl.BlockSpec((tq, rep * dh), lambda g, i, j: (i, g)),
        out_shape=jax.ShapeDtypeStruct((seq, kv_heads * rep * dh), BF16),
        scratch_shapes=[pltpu.VMEM((rep, tq, 1), F32), pltpu.VMEM((rep, tq, 1), F32),
                        pltpu.VMEM((rep, tq, dh), F32)],
        compiler_params=_cp(3), name="gqa_attention")(qg, kg, proj)


def _ret_kernel(gch_ref, q_ref, k_ref, v_ref, dm_ref, xi_ref, ze_ref, *rest, chunk, n_sub, reverse, final):
    if final:
        yb_ref, gate_ref, gain_ref, o_ref, st_ref = rest
    else:
        o_ref, st_ref = rest
    h, n = pl.program_id(0), pl.program_id(1)

    @pl.when(n == 0)
    def _():
        st_ref[...] = jnp.zeros(st_ref.shape, F32)

    dm, xi, ze = dm_ref[0], xi_ref[0], ze_ref[0]
    g_chunk = gch_ref[h]
    for c in (range(n_sub - 1, -1, -1) if reverse else range(n_sub)):
        sl = slice(c * chunk, (c + 1) * chunk)
        q, k, v = q_ref[sl, :], k_ref[sl, :], v_ref[sl, :]
        s = lax.dot_general(q, k, (((1,), (1,)), ((), ())), preferred_element_type=F32) * dm
        st = st_ref[...]
        y = jnp.dot(s.astype(BF16), v, preferred_element_type=F32)
        y = y + jnp.dot((q.astype(F32) * xi).astype(BF16), st.astype(BF16), preferred_element_type=F32)
        kz = (k.astype(F32) * ze).astype(BF16)
        st_ref[...] = g_chunk * st + lax.dot_general(kz, v, (((0,), (0,)), ((), ())),
                                                     preferred_element_type=F32)
        if final:
            gate = gate_ref[sl, :].astype(F32)
            y = _rms(y + yb_ref[sl, :], gain_ref[...]) * (gate * jax.nn.sigmoid(gate))
        o_ref[sl, :] = y.astype(o_ref.dtype)


def _retention_pass(proj, tables, *, seq, heads, q_blk, k_blk, v_blk, gate_blk, reverse, y_bwd=None,
                    gain=None):
    dk, dv, chunk = HEAD_DIM, 2 * HEAD_DIM, RET_CHUNK
    g_chunk, dmask, xi, zeta = tables
    rows = _tile(seq, 8 * chunk)
    n_steps = seq // rows
    final = y_bwd is not None
    blk = (lambda n: n_steps - 1 - n) if reverse else (lambda n: n)
    head_tab = lambda shape: pl.BlockSpec((1,) + shape, lambda h, n: (h, 0, 0))
    in_specs = [pl.BlockSpec(memory_space=pltpu.SMEM),
                pl.BlockSpec((rows, dk), lambda h, n: (blk(n), q_blk + h)),
                pl.BlockSpec((rows, dk), lambda h, n: (blk(n), k_blk + h)),
                pl.BlockSpec((rows, dv), lambda h, n: (blk(n), v_blk + h)),
                head_tab((chunk, chunk)), head_tab((chunk, 1)), head_tab((chunk, 1))]
    args = [g_chunk, proj, proj, proj, dmask, xi, zeta]
    if final:
        in_specs += [pl.BlockSpec((rows, dv), lambda h, n: (blk(n), h)),
                     pl.BlockSpec((rows, dv), lambda h, n: (blk(n), gate_blk + h)),
                     pl.BlockSpec((1, dv), lambda h, n: (0, 0))]
        args += [y_bwd, proj, gain.reshape(1, dv)]
    kern = functools.partial(_ret_kernel, chunk=chunk, n_sub=rows // chunk, reverse=reverse, final=final)
    return pl.pallas_call(
        kern, grid=(heads, n_steps), in_specs=in_specs,
        out_specs=pl.BlockSpec((rows, dv), lambda h, n: (blk(n), h)),
        out_shape=jax.ShapeDtypeStruct((seq, heads * dv), BF16 if final else F32),
        scratch_shapes=[pltpu.VMEM((dk, dv), F32)],
        compiler_params=_cp(2), name="retention")(*args)


def _retention_tables(log_g, reverse):
    c = RET_CHUNK
    idx = jnp.arange(c, dtype=F32)
    diff = idx[:, None] - idx[None, :]
    if reverse:
        diff = -diff
        mask = diff > 0
        xi = jnp.exp((c - idx)[None, :] * log_g[:, None])
        zeta = jnp.exp(idx[None, :] * log_g[:, None])
    else:
        mask = diff >= 0
        xi = jnp.exp((idx + 1.0)[None, :] * log_g[:, None])
        zeta = jnp.exp((c - 1 - idx)[None, :] * log_g[:, None])
    dmask = jnp.where(mask[None], jnp.exp(jnp.where(mask, diff, 0.0)[None] * log_g[:, None, None]), 0.0)
    return jnp.exp(c * log_g), dmask, xi[:, :, None], zeta[:, :, None]


def _dft_tables(seq):
    n = 2 * seq
    n2 = DFT_N2
    n1 = n // n2
    h1 = n1 // 2
    nk1 = n1 // 2 + 1
    k1p = -(-nk1 // 16) * 16
    a2 = np.arange(n2)[:, None, None]
    k1 = np.arange(nk1)[None, :, None]
    a1 = np.arange(h1)[None, None, :]
    theta = 2.0 * np.pi * ((a1 * k1 % n1) / n1 + (a2 * k1 % n) / n)
    fwd = np.zeros((n2, 2 * k1p, h1))
    fwd[:, :nk1] = np.cos(theta)
    fwd[:, k1p:k1p + nk1] = -np.sin(theta)
    weight = np.where((np.arange(nk1) == 0) | (np.arange(nk1) == n1 // 2), 1.0, 2.0)[None, :, None]
    inv = np.zeros((n2, h1, 2 * k1p))
    inv[:, :, :nk1] = np.transpose(weight * np.cos(theta), (0, 2, 1))
    inv[:, :, k1p:k1p + nk1] = np.transpose(-weight * np.sin(theta), (0, 2, 1))
    ang = 2.0 * np.pi * (np.arange(n2)[:, None] * np.arange(n2)[None, :] % n2) / n2
    c, s = np.cos(ang), np.sin(ang)
    f_blk = np.block([[c, s], [-s, c]])
    fi_blk = np.block([[c, -s], [s, c]])
    as_bf16 = lambda a: jnp.asarray(a.astype(BF16))
    return dict(n=n, n1=n1, h1=h1, nk1=nk1, k1p=k1p, fwd=as_bf16(fwd), inv=as_bf16(inv),
                f_blk=as_bf16(f_blk), fi_blk=as_bf16(fi_blk))


def _hy_filter_kernel(feat_ref, w1_ref, b1_ref, w2_ref, b2_ref, w3_ref, fr_ref, dl_ref, h_ref, l1_ref,
                      *, width):
    i = pl.program_id(0)
    feats = feat_ref[...]
    fr = fr_ref[...]
    hid = jnp.sin(fr * (jnp.dot(feats.astype(BF16), w1_ref[...], preferred_element_type=F32) + b1_ref[...]))
    hid = jnp.sin(fr * (jnp.dot(hid.astype(BF16), w2_ref[...], preferred_element_type=F32) + b2_ref[...]))
    h = jnp.dot(hid.astype(BF16), w3_ref[...], preferred_element_type=F32)
    window = jnp.exp(-feats[:, 0:1] * dl_ref[...])
    n_groups = h.shape[1] // width
    row = lax.broadcasted_iota(jnp.int32, (h.shape[0], width), 0)
    not_lag0 = jnp.logical_or(row > 0, i > 0)

    @pl.when(i == 0)
    def _():
        l1_ref[...] = jnp.zeros(l1_ref.shape, F32)

    for g in range(n_groups):
        sl = slice(g * width, (g + 1) * width)
        hg = h[:, sl] * window
        if g >= n_groups // 2:
            hg = jnp.where(not_lag0, hg, 0.0)
        h_ref[:, sl] = hg.astype(h_ref.dtype)
        l1_ref[:, sl] += jnp.sum(jnp.abs(hg), axis=0, keepdims=True)


def _hy_filters(feats, deltas, w1, b1, w2, b2, w3, freq, *, seq, width):
    hidden = w1.shape[1]
    emb = -(-w1.shape[0] // 128) * 128
    feats = jnp.pad(feats, ((0, 0), (0, emb - feats.shape[1])))
    w1 = jnp.pad(w1, ((0, emb - w1.shape[0]), (0, 0)))
    cols = w3.shape[1]
    tl = _tile(seq, 512)
    full = lambda shape: pl.BlockSpec(shape, lambda i: (0,) * len(shape))
    return pl.pallas_call(
        functools.partial(_hy_filter_kernel, width=width), grid=(seq // tl,),
        in_specs=[pl.BlockSpec((tl, emb), lambda i: (i, 0)), full((emb, hidden)), full((1, hidden)),
                  full((hidden, hidden)), full((1, hidden)), full((hidden, cols)), full((1, hidden)),
                  full((1, width))],
        out_specs=[pl.BlockSpec((tl, cols), lambda i: (i, 0)), full((1, cols))],
        out_shape=[jax.ShapeDtypeStruct((seq, cols), BF16), jax.ShapeDtypeStruct((1, cols), F32)],
        compiler_params=_cp(1), name="hy_filters")(
            feats, w1.astype(BF16), b1.reshape(1, hidden), w2.astype(BF16), b2.reshape(1, hidden),
            w3.astype(BF16), freq.reshape(1, hidden), deltas.reshape(1, width))


def _hy_short_conv_kernel(u_ref, up_ref, un_ref, w_ref, b_ref, x0_ref, x1_ref, v_ref, vb_ref):
    i = pl.program_id(0)
    u = u_ref[...]
    tm = u.shape[0]
    row = lax.broadcasted_iota(jnp.int32, u.shape, 0)
    prev_row = jnp.where(i > 0, up_ref[7:8, :], 0.0)
    next_row = jnp.where(i < pl.num_programs(0) - 1, un_ref[0:1, :], 0.0)
    before = jnp.where(row == 0, prev_row, pltpu.roll(u, 1, 0))
    after = jnp.where(row == tm - 1, next_row, pltpu.roll(u, tm - 1, 0))
    y = before * w_ref[0:1, :] + u * w_ref[1:2, :] + after * w_ref[2:3, :] + b_ref[...]
    wd = x0_ref.shape[1]
    x0_ref[...] = y[:, :wd]
    x1_ref[...] = y[:, wd:2 * wd]
    v_ref[...] = y[:, 2 * wd:]
    vb_ref[...] = y[:, 2 * wd:].astype(vb_ref.dtype)


def _hy_short_conv(u, w, b, *, seq, width):
    cols = u.shape[1]
    tm = _tile(seq, 256)
    nb8 = tm // 8
    last8 = seq // 8 - 1
    out = pl.BlockSpec((tm, width), lambda i: (i, 0))
    return pl.pallas_call(
        _hy_short_conv_kernel, grid=(seq // tm,),
        in_specs=[pl.BlockSpec((tm, cols), lambda i: (i, 0)),
                  pl.BlockSpec((8, cols), lambda i: (jnp.maximum(i * nb8 - 1, 0), 0)),
                  pl.BlockSpec((8, cols), lambda i: (jnp.minimum((i + 1) * nb8, last8), 0)),
                  pl.BlockSpec((3, cols), lambda i: (0, 0)), pl.BlockSpec((1, cols), lambda i: (0, 0))],
        out_specs=[out, out, out, out],
        out_shape=[jax.ShapeDtypeStruct((seq, width), F32)] * 3 + [jax.ShapeDtypeStruct((seq, width), BF16)],
        compiler_params=_cp(1), name="hy_short_conv")(u, u, u, w, b.reshape(1, cols))


def _dft_a_kernel(x_ref, g_ref, re_ref, im_ref, *, n_sub, cols, k1p):
    for s in range(n_sub):
        sl = slice(s * cols, (s + 1) * cols)
        t = jnp.dot(g_ref[s], x_ref[:, sl], preferred_element_type=F32)
        re_ref[:, sl] = t[:k1p].astype(re_ref.dtype)
        im_ref[:, sl] = t[k1p:].astype(im_ref.dtype)


def _dft_a(x, tabs, *, cols):
    h1, k1p = tabs["h1"], tabs["k1p"]
    xv = x.reshape(h1, DFT_N2 * cols)
    n_sub = max(1, min(DFT_N2, 8192 // cols))
    spec_o = pl.BlockSpec((k1p, n_sub * cols), lambda i: (0, i))
    out = jax.ShapeDtypeStruct((k1p, DFT_N2 * cols), BF16)
    re, im = pl.pallas_call(
        functools.partial(_dft_a_kernel, n_sub=n_sub, cols=cols, k1p=k1p), grid=(DFT_N2 // n_sub,),
        in_specs=[pl.BlockSpec((h1, n_sub * cols), lambda i: (0, i)),
                  pl.BlockSpec((n_sub, 2 * k1p, h1), lambda i: (i, 0, 0))],
        out_specs=[spec_o, spec_o], out_shape=[out, out],
        compiler_params=_cp(1), name="dft_outer")(xv, tabs["fwd"])
    return re.reshape(k1p, DFT_N2, cols), im.reshape(k1p, DFT_N2, cols)


def _hy_filter_spec_kernel(fr_ref, fi_ref, br_ref, bi_ref, f_ref, w_ref, kr_ref, ki_ref):
    n2 = fr_ref.shape[1]
    f = f_ref[...]
    sf = jnp.dot(f, jnp.concatenate([fr_ref[0], fi_ref[0]], axis=0), preferred_element_type=F32)
    sb = jnp.dot(f, jnp.concatenate([br_ref[0], bi_ref[0]], axis=0), preferred_element_type=F32)
    w = w_ref[...]
    kr_ref[0] = (sf[:n2] + sb[:n2]) * w
    ki_ref[0] = (sf[n2:] - sb[n2:]) * w


def _hy_filter_spectrum(h_re, h_im, tabs, wnorm, *, cols):
    nk1 = tabs["nk1"]
    tc = _tile(cols, 1024)
    nb = cols // tc
    fwd = pl.BlockSpec((1, DFT_N2, tc), lambda k, j: (k, 0, j))
    bwd = pl.BlockSpec((1, DFT_N2, tc), lambda k, j: (k, 0, nb + j))
    out = jax.ShapeDtypeStruct((nk1, DFT_N2, cols), F32)
    return pl.pallas_call(
        _hy_filter_spec_kernel, grid=(nk1, nb),
        in_specs=[fwd, fwd, bwd, bwd, pl.BlockSpec((2 * DFT_N2, 2 * DFT_N2), lambda k, j: (0, 0)),
                  pl.BlockSpec((1, tc), lambda k, j: (0, j))],
        out_specs=[fwd, fwd], out_shape=[out, out],
        compiler_params=_cp(2), name="hy_filter_spectrum")(h_re, h_im, h_re, h_im, tabs["f_blk"], wnorm)


def _hy_conv_kernel(tr_ref, ti_ref, kr_ref, ki_ref, f_ref, fi_ref, cr_ref, ci_ref, *, nk1):
    k1 = pl.program_id(0)
    n2 = tr_ref.shape[1]

    @pl.when(k1 < nk1)
    def _():
        s = jnp.dot(f_ref[...], jnp.concatenate([tr_ref[0], ti_ref[0]], axis=0), preferred_element_type=F32)
        sr, si = s[:n2], s[n2:]
        kr, ki = kr_ref[0], ki_ref[0]
        y = jnp.concatenate([sr * kr - si * ki, sr * ki + si * kr], axis=0).astype(BF16)
        c = jnp.dot(fi_ref[...], y, preferred_element_type=F32)
        cr_ref[0] = c[:n2].astype(cr_ref.dtype)
        ci_ref[0] = c[n2:].astype(ci_ref.dtype)

    @pl.when(k1 >= nk1)
    def _():
        cr_ref[...] = jnp.zeros(cr_ref.shape, cr_ref.dtype)
        ci_ref[...] = jnp.zeros(ci_ref.shape, ci_ref.dtype)


def _hy_conv(t_re, t_im, k_re, k_im, tabs, *, order, cols):
    nk1, k1p = tabs["nk1"], tabs["k1p"]
    tc = _tile(cols, 1024)
    nb = cols // tc
    sig = pl.BlockSpec((1, DFT_N2, tc), lambda k, j: (k, 0, j))
    flt = pl.BlockSpec((1, DFT_N2, tc), lambda k, j: (jnp.minimum(k, nk1 - 1), 0, order * nb + j))
    mat = pl.BlockSpec((2 * DFT_N2, 2 * DFT_N2), lambda k, j: (0, 0))
    out = jax.ShapeDtypeStruct((k1p, DFT_N2, cols), BF16)
    return pl.pallas_call(
        functools.partial(_hy_conv_kernel, nk1=nk1), grid=(k1p, nb),
        in_specs=[sig, sig, flt, flt, mat, mat], out_specs=[sig, sig], out_shape=[out, out],
        compiler_params=_cp(2), name="hy_conv")(t_re, t_im, k_re, k_im, tabs["f_blk"], tabs["fi_blk"])


def _idft_a_kernel(cr_ref, ci_ref, g_ref, x_ref, z_ref, d_ref, o_ref, ob_ref, *, n_sub, cols):
    for s in range(n_sub):
        sl = slice(s * cols, (s + 1) * cols)
        c = jnp.concatenate([cr_ref[:, sl], ci_ref[:, sl]], axis=0)
        y = jnp.dot(g_ref[s], c, preferred_element_type=F32)
        out = x_ref[:, sl] * (y + z_ref[:, sl] * d_ref[...])
        o_ref[:, sl] = out
        ob_ref[:, sl] = out.astype(ob_ref.dtype)


def _idft_a(c_re, c_im, tabs, x_mul, z_prev, d_term, *, seq, cols):
    h1, k1p = tabs["h1"], tabs["k1p"]
    n_sub = max(1, min(DFT_N2, 8192 // cols))
    wide = n_sub * cols
    spec_c = pl.BlockSpec((k1p, wide), lambda i: (0, i))
    spec_x = pl.BlockSpec((h1, wide), lambda i: (0, i))
    view = lambda a: a.reshape(h1, DFT_N2 * cols)
    o, ob = pl.pallas_call(
        functools.partial(_idft_a_kernel, n_sub=n_sub, cols=cols), grid=(DFT_N2 // n_sub,),
        in_specs=[spec_c, spec_c, pl.BlockSpec((n_sub, h1, 2 * k1p), lambda i: (i, 0, 0)), spec_x, spec_x,
                  pl.BlockSpec((1, cols), lambda i: (0, 0))],
        out_specs=[spec_x, spec_x],
        out_shape=[jax.ShapeDtypeStruct((h1, DFT_N2 * cols), F32),
                   jax.ShapeDtypeStruct((h1, DFT_N2 * cols), BF16)],
        compiler_params=_cp(1), name="idft_outer")(
            c_re.reshape(k1p, DFT_N2 * cols), c_im.reshape(k1p, DFT_N2 * cols), tabs["inv"],
            view(x_mul), view(z_prev), d_term.reshape(1, cols))
    return o.reshape(seq, cols), ob.reshape(seq, cols)


def _hyena(hy_u, conv_w, conv_b, filt_params, d_term, feats, deltas, tabs, *, seq, width):
    hwin, l1 = _hy_filters(feats, deltas, *filt_params, seq=seq, width=width)
    n_cols = HY_ORDER * width
    l1 = l1[0, :n_cols] + l1[0, n_cols:]
    wnorm = (1.0 / (tabs["n"] * (l1 + EPS))).reshape(1, n_cols)
    h_re, h_im = _dft_a(hwin, tabs, cols=2 * n_cols)
    k_re, k_im = _hy_filter_spectrum(h_re, h_im, tabs, wnorm, cols=n_cols)
    x0, x1, v, v_bf = _hy_short_conv(hy_u, conv_w, conv_b, seq=seq, width=width)
    z, z_bf = v, v_bf
    for order, x_mul in enumerate((x0, x1)):
        t_re, t_im = _dft_a(z_bf, tabs, cols=width)
        c_re, c_im = _hy_conv(t_re, t_im, k_re, k_im, tabs, order=order, cols=width)
        z, z_bf = _idft_a(c_re, c_im, tabs, x_mul, z, d_term[order], seq=seq, cols=width)
    return z_bf


def _merge_kernel(gl_ref, oa_ref, ob_ref, oc_ref, od_ref, wg_ref, wb_ref, bg_ref, o_ref):
    gl = gl_ref[...]
    acc = None
    for i, o in enumerate((oa_ref, ob_ref, oc_ref, od_ref)):
        gate = jax.nn.sigmoid(jnp.dot(gl, wg_ref[i], preferred_element_type=F32) + bg_ref[i])
        term = gate * jnp.dot(o[...], wb_ref[i], preferred_element_type=F32)
        acc = term if acc is None else acc + term
    o_ref[...] = acc.astype(o_ref.dtype)


def _merge(proj, branches, w_gate, b_gate, w_branch, *, gate_blk, tm=512, tn=1024):
    m, unit = branches[0].shape
    nbr, rank, n = w_gate.shape
    tm, tn = _tile(m, tm), _tile(n, tn)
    br = pl.BlockSpec((tm, unit), lambda j, i: (i, 0))
    return pl.pallas_call(
        _merge_kernel, grid=(n // tn, m // tm),
        in_specs=[pl.BlockSpec((tm, rank), lambda j, i: (i, gate_blk)), br, br, br, br,
                  pl.BlockSpec((nbr, rank, tn), lambda j, i: (0, 0, j)),
                  pl.BlockSpec((nbr, unit, tn), lambda j, i: (0, 0, j)),
                  pl.BlockSpec((nbr, 1, tn), lambda j, i: (0, 0, j))],
        out_specs=pl.BlockSpec((tm, tn), lambda j, i: (i, j)),
        out_shape=jax.ShapeDtypeStruct((m, n), BF16),
        compiler_params=_cp(2), name="merge")(proj, *branches, w_gate.astype(BF16), w_branch.astype(BF16),
                                            b_gate.reshape(nbr, 1, n))


def _xattn_kernel(h_ref, x_ref, wq_ref, kv_ref, wo_ref, gp_ref, gn_ref, xo_ref, ho_ref, *, heads, dh, qscale):
    q = (jnp.dot(h_ref[...], wq_ref[...], preferred_element_type=F32) * qscale).astype(BF16)
    kv = kv_ref[...]
    outs = []
    for hh in range(heads):
        k = kv[:, hh * dh:(hh + 1) * dh]
        v = kv[:, (heads + hh) * dh:(heads + hh + 1) * dh]
        s = lax.dot_general(q[:, hh * dh:(hh + 1) * dh], k, (((1,), (1,)), ((), ())),
                            preferred_element_type=F32)
        p = jnp.exp2(s - jnp.max(s, axis=-1, keepdims=True))
        o = jnp.dot(p.astype(BF16), v, preferred_element_type=F32) / jnp.sum(p, axis=-1, keepdims=True)
        outs.append(o.astype(BF16))
    y = jnp.dot(jnp.concatenate(outs, axis=1), wo_ref[...], preferred_element_type=F32)
    xn = x_ref[...] + _rms(y, gp_ref[...])
    xo_ref[...] = xn
    ho_ref[...] = _rms(xn, gn_ref[...]).astype(ho_ref.dtype)


def _cross_attention(h, x, kv, wq, wo, g_post, g_next):
    m, d = x.shape
    dh, heads = HEAD_DIM, XA_HEADS
    n_mem = kv.shape[0]
    tm = _tile(m, 256)
    row = pl.BlockSpec((tm, d), lambda i: (i, 0))
    vec = pl.BlockSpec((1, d), lambda i: (0, 0))
    full = lambda shape: pl.BlockSpec(shape, lambda i: (0, 0))
    kern = functools.partial(_xattn_kernel, heads=heads, dh=dh, qscale=dh ** -0.5 * LOG2E)
    return pl.pallas_call(
        kern, grid=(m // tm,),
        in_specs=[row, row, full((d, heads * dh)), full((n_mem, 2 * heads * dh)), full((heads * dh, d)),
                  vec, vec],
        out_specs=[row, row],
        out_shape=[jax.ShapeDtypeStruct((m, d), F32), jax.ShapeDtypeStruct((m, d), BF16)],
        compiler_params=_cp(1), name="cross_attention")(
            h, x, wq.astype(BF16), kv, wo.astype(BF16), g_post.reshape(1, d), g_next.reshape(1, d))


def _rope_tables(seq):
    dh = HEAD_DIM
    n_rows = seq // GRID_W
    row = jnp.broadcast_to(jnp.arange(n_rows, dtype=F32)[:, None], (n_rows, GRID_W)).reshape(seq)
    col = jnp.broadcast_to(jnp.arange(GRID_W, dtype=F32)[None, :], (n_rows, GRID_W)).reshape(seq)
    axis_dim = dh // 2
    inv_freq = ROPE_THETA ** (-jnp.arange(0, axis_dim, 2, dtype=F32) / axis_dim)
    ang = jnp.stack([row[:, None] * inv_freq, col[:, None] * inv_freq], axis=1)
    ang = jnp.broadcast_to(ang[:, :, None, :], (seq, 2, 2, axis_dim // 2)).reshape(seq, dh)
    sign = jnp.where((jnp.arange(dh) % (dh // 2)) < dh // 4, -1.0, 1.0).astype(F32)
    return jnp.cos(ang), jnp.sin(ang) * sign[None, :]


def _hyena_tables(seq, width):
    bands = (HY_EMB - 1) // 2
    pos = jnp.arange(seq, dtype=F32)
    t = pos / (seq - 1)
    w = 2.0 * math.pi * pos / seq
    f = jnp.linspace(1e-4, bands - 1, bands, dtype=F32)
    ang = w[:, None] * f[None]
    feats = jnp.concatenate([t[:, None], jnp.cos(ang), -jnp.sin(ang)], axis=-1)
    deltas = jnp.abs(jnp.linspace(math.log(HY_DECAY_TARGET) / HY_SLOW_PCT,
                                  math.log(HY_DECAY_TARGET) / HY_FAST_PCT, width, dtype=F32))
    return feats, deltas


def kernel(x, mem, ffn1_pre_norm, ffn1_w1, ffn1_w3, ffn1_w2, ffn1_post_norm, mix_pre_norm, w_in, diff_lambda, diff_norm, gqa_q_norm, gqa_k_norm, ret_decay_logit, ret_norm, hy_conv_w, hy_conv_b, hy_w1, hy_b1, hy_w2, hy_b2, hy_w3, hy_sin_freq, hy_filter_bias, w_gate_up, b_gate, w_branch, w_out, mix_post_norm, xa_pre_norm, xa_mem_norm, xa_wq, xa_wkv, xa_wo, xa_post_norm, ffn2_pre_norm, ffn2_w1, ffn2_w3, ffn2_w2, ffn2_post_norm):
    batch, seq, d_model = x.shape
    assert batch == 1
    depth = w_in.shape[0]
    dh = HEAD_DIM
    unit = d_model // 4
    da_heads = unit // (2 * dh)
    gqa_heads = unit // dh
    gqa_kv = gqa_heads // 4
    ret_heads = unit // (2 * dh)
    hy_w = HY_ORDER + 1
    names = ("qa", "ka", "va", "qb", "kb", "vb", "qc", "kc", "vc", "gc", "hy", "gate")
    widths = (da_heads * 2 * dh, da_heads * 2 * dh, da_heads * 2 * dh, gqa_heads * dh, gqa_kv * dh,
              gqa_kv * dh, ret_heads * dh, ret_heads * dh, ret_heads * 2 * dh, ret_heads * 2 * dh,
              hy_w * unit, GATE_RANK)
    assert sum(widths) == w_in.shape[2]
    src = dict(zip(names, np.concatenate([[0], np.cumsum(widths)[:-1]]).tolist()))
    wid = dict(zip(names, widths))
    blk_w = dict(qa=2 * dh, ka=2 * dh, va=2 * dh, qb=wid["qb"], kb=wid["kb"], vb=dh, qc=dh, kc=dh,
                 vc=2 * dh, gc=2 * dh, gate=GATE_RANK)
    order_a = sorted(blk_w, key=lambda nm: -blk_w[nm])
    off, pos = {}, 0
    for nm in order_a:
        assert pos % blk_w[nm] == 0 and wid[nm] % blk_w[nm] == 0, nm
        off[nm] = pos
        pos += wid[nm]
    n_a = pos

    att_scale = dh ** -0.5 * LOG2E
    colscale = jnp.ones((n_a,), F32)
    colscale = colscale.at[off["qa"]:off["qa"] + wid["qa"]].set(att_scale)
    colscale = colscale.at[off["kc"]:off["kc"] + wid["kc"]].set(dh ** -0.5)

    slopes = 2.0 ** (-8.0 * jnp.arange(1, da_heads + 1, dtype=F32) / da_heads)
    rope_cos, rope_sin = _rope_tables(seq)
    hy_feats, hy_deltas = _hyena_tables(seq, unit)
    tabs = _dft_tables(seq)

    xs = x.reshape(seq, d_model)
    mem2 = mem.reshape(mem.shape[1], d_model)
    h = _norm_cast(xs, ffn1_pre_norm[0])
    for l in range(depth):
        xs, h = _ffn(xs, h, ffn1_w1[l], ffn1_w3[l], ffn1_w2[l], ffn1_post_norm[l], mix_pre_norm[l])

        w_l = w_in[l]
        w_a = jnp.concatenate([w_l[:, src[nm]:src[nm] + wid[nm]] for nm in order_a], axis=1).astype(BF16)
        proj = _matmul(h, w_a, BF16, colscale)
        hy_u = _matmul(h, w_l[:, src["hy"]:src["gate"]].astype(BF16), F32)

        lam_init = 0.8 - 0.6 * math.exp(-0.3 * l)
        lp = diff_lambda[l]
        lam = jnp.exp(jnp.sum(lp[0] * lp[1])) - jnp.exp(jnp.sum(lp[2] * lp[3])) + lam_init
        scal = jnp.concatenate([lam.reshape(1), slopes * LOG2E]).astype(F32)
        oa = _diff_attention(proj, scal, diff_norm[l], seq=seq, heads=da_heads,
                             q_blk=off["qa"] // (2 * dh), k_blk=off["ka"] // (2 * dh),
                             v_blk=off["va"] // (2 * dh), post_scale=1.0 - lam_init)

        qg, kg = _gqa_prep(proj, rope_cos, rope_sin, gqa_q_norm[l], gqa_k_norm[l], seq=seq,
                           q_w=wid["qb"], kv_w=wid["kb"], q_blk=off["qb"] // wid["qb"],
                           k_blk=off["kb"] // wid["kb"], qscale=att_scale)
        ob = _gqa_attention(qg, kg, proj, seq=seq, kv_heads=gqa_kv, rep=gqa_heads // gqa_kv,
                            v_blk=off["vb"] // dh)

        log_g = -jax.nn.softplus(-ret_decay_logit[l].astype(F32))
        ret_args = dict(seq=seq, heads=ret_heads, q_blk=off["qc"] // dh, k_blk=off["kc"] // dh,
                        v_blk=off["vc"] // (2 * dh), gate_blk=off["gc"] // (2 * dh))
        y_bwd = _retention_pass(proj, _retention_tables(log_g[1], True), reverse=True, **ret_args)
        oc = _retention_pass(proj, _retention_tables(log_g[0], False), reverse=False, y_bwd=y_bwd,
                             gain=ret_norm[l], **ret_args)

        od = _hyena(hy_u, hy_conv_w[l], hy_conv_b[l],
                    (hy_w1[l], hy_b1[l], hy_w2[l], hy_b2[l], hy_w3[l], hy_sin_freq[l]),
                    hy_filter_bias[l], hy_feats, hy_deltas, tabs, seq=seq, width=unit)

        merged = _merge(proj, (oa, ob, oc, od), w_gate_up[l], b_gate[l], w_branch[l],
                        gate_blk=off["gate"] // GATE_RANK)
        y = _matmul(merged, w_out[l].astype(BF16), F32)
        xs, h = _resid_norm(xs, y, mix_post_norm[l], 1.0, xa_pre_norm[l])

        mem_n = _norm_cast(mem2, xa_mem_norm[l])
        kv = _matmul(mem_n, xa_wkv[l].astype(BF16), BF16)
        xs, h = _cross_attention(h, xs, kv, xa_wq[l], xa_wo[l], xa_post_norm[l], ffn2_pre_norm[l])

        g_next = ffn1_pre_norm[l + 1] if l + 1 < depth else ffn2_pre_norm[l]
        xs, h = _ffn(xs, h, ffn2_w1[l], ffn2_w3[l], ffn2_w2[l], ffn2_post_norm[l], g_next)
    return xs.reshape(batch, seq, d_model)
```

```python
import functools
import math

import numpy as np
import jax
import jax.numpy as jnp
from jax import lax
from jax.experimental import pallas as pl
from jax.experimental.pallas import tpu as pltpu

F32 = jnp.float32
BF16 = jnp.bfloat16

EPS = 1e-6
HEAD_DIM = 128
GATE_RANK = 512
GRID_W = 64
ROPE_THETA = 10000.0
RET_CHUNK = 128
XA_HEADS = 4
HY_ORDER = 2
HY_EMB = 33
HY_DECAY_TARGET = 1e-2
HY_FAST_PCT = 0.3
HY_SLOW_PCT = 1.5
LOG2E = math.log2(math.e)
DFT_N2 = 256
NEG_BIG = -1e30

VMEM_LIMIT_BYTES = 56 * 1024 * 1024


def _cp(n_axes):
    return pltpu.CompilerParams(dimension_semantics=("arbitrary",) * n_axes,
                                vmem_limit_bytes=VMEM_LIMIT_BYTES)


def _tile(n, pref, align=128):
    for t in range(min(n, pref), 0, -1):
        if n % t == 0 and t % align == 0:
            return t
    raise ValueError((n, pref, align))


def _rms(x, gain):
    return x * lax.rsqrt(jnp.mean(x * x, axis=-1, keepdims=True) + EPS) * gain


def _norm_kernel(x_ref, g_ref, h_ref):
    h_ref[...] = _rms(x_ref[...], g_ref[...]).astype(h_ref.dtype)


def _norm_cast(x, gain):
    m, d = x.shape
    tm = _tile(m, 256)
    row = pl.BlockSpec((tm, d), lambda i: (i, 0))
    vec = pl.BlockSpec((1, d), lambda i: (0, 0))
    return pl.pallas_call(
        _norm_kernel, grid=(m // tm,), in_specs=[row, vec], out_specs=row,
        out_shape=jax.ShapeDtypeStruct((m, d), BF16), compiler_params=_cp(1),
        name="norm_cast")(x, gain.reshape(1, d))


def _resid_norm_kernel(x_ref, y_ref, gp_ref, gn_ref, xo_ref, ho_ref, *, alpha):
    xn = x_ref[...] + alpha * _rms(y_ref[...], gp_ref[...])
    xo_ref[...] = xn
    ho_ref[...] = _rms(xn, gn_ref[...]).astype(ho_ref.dtype)


def _resid_norm(x, y, g_post, alpha, g_next):
    m, d = x.shape
    tm = _tile(m, 256)
    row = pl.BlockSpec((tm, d), lambda i: (i, 0))
    vec = pl.BlockSpec((1, d), lambda i: (0, 0))
    return pl.pallas_call(
        functools.partial(_resid_norm_kernel, alpha=alpha), grid=(m // tm,),
        in_specs=[row, row, vec, vec], out_specs=[row, row],
        out_shape=[jax.ShapeDtypeStruct((m, d), F32), jax.ShapeDtypeStruct((m, d), BF16)],
        compiler_params=_cp(1), name="resid_norm")(x, y, g_post.reshape(1, d), g_next.reshape(1, d))


def _mm_kernel(a_ref, w_ref, s_ref, o_ref):
    acc = jnp.dot(a_ref[...], w_ref[...], preferred_element_type=F32)
    o_ref[...] = (acc * s_ref[...]).astype(o_ref.dtype)


def _matmul(a, w, out_dtype, colscale=None, tm=512, tn=1024):
    m, k = a.shape
    n = w.shape[1]
    tm, tn = _tile(m, tm), _tile(n, tn)
    if colscale is None:
        colscale = jnp.ones((n,), F32)
    return pl.pallas_call(
        _mm_kernel, grid=(n // tn, m // tm),
        in_specs=[pl.BlockSpec((tm, k), lambda j, i: (i, 0)),
                  pl.BlockSpec((k, tn), lambda j, i: (0, j)),
                  pl.BlockSpec((1, tn), lambda j, i: (0, j))],
        out_specs=pl.BlockSpec((tm, tn), lambda j, i: (i, j)),
        out_shape=jax.ShapeDtypeStruct((m, n), out_dtype),
        compiler_params=_cp(2), name="matmul")(a, w, colscale.reshape(1, n).astype(F32))


def _ffn_up_kernel(h_ref, w1_ref, w3_ref, o_ref):
    h = h_ref[...]
    a = jnp.dot(h, w1_ref[...], preferred_element_type=F32)
    b = jnp.dot(h, w3_ref[...], preferred_element_type=F32)
    o_ref[...] = (a * jax.nn.sigmoid(a) * b).astype(o_ref.dtype)


def _ffn_up(h, w1, w3, tm=512, tn=512):
    m, k = h.shape
    n = w1.shape[1]
    tm, tn = _tile(m, tm), _tile(n, tn)
    wspec = pl.BlockSpec((k, tn), lambda j, i: (0, j))
    return pl.pallas_call(
        _ffn_up_kernel, grid=(n // tn, m // tm),
        in_specs=[pl.BlockSpec((tm, k), lambda j, i: (i, 0)), wspec, wspec],
        out_specs=pl.BlockSpec((tm, tn), lambda j, i: (i, j)),
        out_shape=jax.ShapeDtypeStruct((m, n), BF16),
        compiler_params=_cp(2), name="ffn_up")(h, w1, w3)


def _ffn(x, h, w1, w3, w2, g_post, g_next):
    u = _ffn_up(h, w1.astype(BF16), w3.astype(BF16))
    y = _matmul(u, w2.astype(BF16), F32)
    return _resid_norm(x, y, g_post, 0.5, g_next)


FLASH_SUBSTEPS = 2
FLASH_KEY_BLOCK = 2048


def _alibi_mixed_bases(tq, tk):
    step = math.gcd(tq, tk)
    return [b for b in range(-(tq // step) * step, tk + step, step) if 1 - tq < b < tk - 1], step


def _flash_kernel(*refs, n_maps, dh, tq, tk, nq, nk, diff, post_scale):
    if diff:
        scal_ref, q_ref, k_ref, v_ref, g_ref, o_ref, s_buf, p_buf, al_buf, m_ref, acc_ref, l_ref, tbl_ref = refs
        mixed, base_step = _alibi_mixed_bases(tq, tk)
    else:
        q_ref, k_ref, v_ref, o_ref, s_buf, p_buf, al_buf, m_ref, acc_ref = refs
    h, g = pl.program_id(0), pl.program_id(1)
    n_sub = FLASH_SUBSTEPS
    total = nq * nk

    @pl.when(g == 0)
    def _():
        s_buf[...] = jnp.zeros(s_buf.shape, F32)
        p_buf[...] = jnp.zeros(p_buf.shape, BF16)
        al_buf[...] = jnp.ones(al_buf.shape, F32)
        m_ref[...] = jnp.full(m_ref.shape, NEG_BIG, F32)
        acc_ref[...] = jnp.zeros(acc_ref.shape, F32)
        if diff:
            l_ref[...] = jnp.zeros(l_ref.shape, F32)
            d = lax.broadcasted_iota(jnp.int32, (tq, tk), 0) - lax.broadcasted_iota(jnp.int32, (tq, tk), 1)
            slope = scal_ref[1 + h]
            tbl_ref[0] = d.astype(F32) * slope
            tbl_ref[1] = (-d).astype(F32) * slope
            for i, mb in enumerate(mixed):
                tbl_ref[2 + i] = jnp.abs(d + mb).astype(F32) * slope

    for sub in range(n_sub):
        other = 1 - sub
        t = n_sub * g + sub
        for c in range(n_maps):
            k_lo = c * dh if diff else 0
            s_buf[sub, c] = lax.dot_general(q_ref[:, c * dh:(c + 1) * dh],
                                            k_ref[sub * tk:(sub + 1) * tk, k_lo:k_lo + dh],
                                            (((1,), (1,)), ((), ())), preferred_element_type=F32)
        u = t - 1
        valid = jnp.logical_and(u >= 0, u < total)
        uc = jnp.clip(u, 0, total - 1)
        qb, kb = uc // nk, uc % nk
        first = kb == 0
        par = qb % 2
        if diff:
            base = qb * tq - kb * tk
            below, above = base >= tk - 1, base <= 1 - tq
            sel = jnp.where(below, 0, jnp.where(above, 1, 2 + (base - mixed[0]) // base_step))
            bias = tbl_ref[jnp.clip(sel, 0, 1 + len(mixed))]
            basef = (jnp.zeros((tq, 1), jnp.int32) + base).astype(F32) * scal_ref[1 + h]
            cst = jnp.where(below, -basef, jnp.where(above, basef, 0.0))
        else:
            cst = jnp.zeros((tq, 1), F32)
        cst = jnp.where(valid, cst, -jnp.inf)
        for c in range(n_maps):
            s = s_buf[other, c]
            if diff:
                s = s - bias
            m_prev = jnp.where(first, NEG_BIG, m_ref[par, c])
            m_new = jnp.maximum(m_prev, jnp.max(s, axis=-1, keepdims=True) + cst)
            alpha = jnp.exp2(m_prev - m_new)
            p = jnp.exp2(s - (m_new - cst))
            if diff:
                l_ref[par, c] = alpha * l_ref[par, c] + jnp.sum(p, axis=-1, keepdims=True)
            m_ref[par, c] = m_new
            al_buf[other, c] = alpha
            p_buf[other, c] = p.astype(BF16)
        v = v_ref[sub * tk:(sub + 1) * tk, :]
        for c in range(n_maps):
            acc_ref[c] = al_buf[sub, c] * acc_ref[c] + jnp.dot(p_buf[sub, c], v, preferred_element_type=F32)

    w_last = n_sub * g - 1

    @pl.when(jnp.logical_and(w_last >= 0, w_last % nk == nk - 1))
    def _():
        if diff:
            par_w = (w_last // nk) % 2
            o = acc_ref[0] / l_ref[par_w, 0] - scal_ref[0] * (acc_ref[1] / l_ref[par_w, 1])
            o_ref[...] = (_rms(o, g_ref[...]) * post_scale).astype(o_ref.dtype)
        else:
            for r in range(n_maps):
                a = acc_ref[r]
                o_ref[:, r * dh:(r + 1) * dh] = (a[:, :dh] / a[:, dh:]).astype(o_ref.dtype)


def _flash_attention(q, k, v, *, seq, heads, n_maps, q_blk, k_blk, v_blk, tq, diff_args=None):
    dh = HEAD_DIM
    n_sub = FLASH_SUBSTEPS
    diff = diff_args is not None
    tq = _tile(seq, tq)
    tk = _tile(seq // n_sub, FLASH_KEY_BLOCK)
    nq, nk = seq // tq, seq // tk
    assert nk % n_sub == 0
    pairs = nk // n_sub
    kern = functools.partial(_flash_kernel, n_maps=n_maps, dh=dh, tq=tq, tk=tk, nq=nq, nk=nk, diff=diff,
                             post_scale=diff_args[2] if diff else None)
    in_specs = [pl.BlockSpec((tq, n_maps * dh), lambda h, g: (jnp.minimum(n_sub * g // nk, nq - 1), q_blk + h)),
                pl.BlockSpec((n_sub * tk, n_maps * dh if diff else dh), lambda h, g: (g % pairs, k_blk + h)),
                pl.BlockSpec((n_sub * tk, 2 * dh), lambda h, g: (jnp.maximum(g - 1, 0) % pairs, v_blk + h))]
    args = [q, k, v]
    scratch = [pltpu.VMEM((n_sub, n_maps, tq, tk), F32), pltpu.VMEM((n_sub, n_maps, tq, tk), BF16),
               pltpu.VMEM((n_sub, n_maps, tq, 1), F32), pltpu.VMEM((2, n_maps, tq, 1), F32),
               pltpu.VMEM((n_maps, tq, 2 * dh), F32)]
    if diff:
        scal, gain, _ = diff_args
        in_specs = [pl.BlockSpec(memory_space=pltpu.SMEM)] + in_specs + [pl.BlockSpec((1, 2 * dh), lambda h, g: (0, 0))]
        args = [scal] + args + [gain.reshape(1, 2 * dh)]
        scratch += [pltpu.VMEM((2, n_maps, tq, 1), F32),
                    pltpu.VMEM((2 + len(_alibi_mixed_bases(tq, tk)[0]), tq, tk), F32)]
    return pl.pallas_call(
        kern, grid=(heads, nq * pairs + 1), in_specs=in_specs,
        out_specs=pl.BlockSpec((tq, n_maps * dh), lambda h, g: (jnp.maximum(n_sub * g - n_sub, 0) // nk, h)),
        out_shape=jax.ShapeDtypeStruct((seq, heads * n_maps * dh), BF16),
        scratch_shapes=scratch, compiler_params=_cp(2),
        name="diff_attention" if diff else "gqa_attention")(*args)


def _gqa_prep_kernel(q_ref, k_ref, v_ref, cos_ref, sin_ref, gq_ref, gk_ref, qo_ref, ko_ref, vo_ref, *, qscale):
    cos = cos_ref[...]
    sin = sin_ref[...]
    dh = cos.shape[-1]
    lane = lax.broadcasted_iota(jnp.int32, cos.shape, 1)
    first_half = (lane % (dh // 2)) < (dh // 4)

    def norm_rope(xh, gain):
        xn = _rms(xh.astype(F32), gain)
        rot = jnp.where(first_half, pltpu.roll(xn, dh - dh // 4, 1), pltpu.roll(xn, dh // 4, 1))
        return xn * cos + rot * sin

    for hh in range(q_ref.shape[1] // dh):
        sl = slice(hh * dh, (hh + 1) * dh)
        qo_ref[:, sl] = (norm_rope(q_ref[:, sl], gq_ref[...]) * qscale).astype(qo_ref.dtype)
    for hh in range(k_ref.shape[1] // dh):
        sl = slice(hh * dh, (hh + 1) * dh)
        ko_ref[:, sl] = norm_rope(k_ref[:, sl], gk_ref[...]).astype(ko_ref.dtype)
        vo_ref[:, 2 * hh * dh:(2 * hh + 1) * dh] = v_ref[:, sl]
        vo_ref[:, (2 * hh + 1) * dh:(2 * hh + 2) * dh] = jnp.ones((v_ref.shape[0], dh), vo_ref.dtype)


def _gqa_prep(proj, cos, sin_signed, gq, gk, *, seq, q_w, kv_w, q_blk, k_blk, v_blk, qscale):
    dh = HEAD_DIM
    tm = _tile(seq, 512)
    vec = pl.BlockSpec((1, dh), lambda i: (0, 0))
    tab = pl.BlockSpec((tm, dh), lambda i: (i, 0))
    return pl.pallas_call(
        functools.partial(_gqa_prep_kernel, qscale=qscale), grid=(seq // tm,),
        in_specs=[pl.BlockSpec((tm, q_w), lambda i: (i, q_blk)),
                  pl.BlockSpec((tm, kv_w), lambda i: (i, k_blk)),
                  pl.BlockSpec((tm, kv_w), lambda i: (i, v_blk)), tab, tab, vec, vec],
        out_specs=[pl.BlockSpec((tm, q_w), lambda i: (i, 0)), pl.BlockSpec((tm, kv_w), lambda i: (i, 0)),
                   pl.BlockSpec((tm, 2 * kv_w), lambda i: (i, 0))],
        out_shape=[jax.ShapeDtypeStruct((seq, q_w), BF16), jax.ShapeDtypeStruct((seq, kv_w), BF16),
                   jax.ShapeDtypeStruct((seq, 2 * kv_w), BF16)],
        compiler_params=_cp(1), name="gqa_prep")(proj, proj, proj, cos, sin_signed, gq.reshape(1, dh),
                                                gk.reshape(1, dh))


def _ret_kernel(gch_ref, q_ref, k_ref, v_ref, dm_ref, xi_ref, ze_ref, *rest, chunk, n_sub, reverse, final):
    if final:
        yb_ref, gate_ref, gain_ref, o_ref, st_ref = rest
    else:
        o_ref, st_ref = rest
    h, n = pl.program_id(0), pl.program_id(1)

    @pl.when(n == 0)
    def _():
        st_ref[...] = jnp.zeros(st_ref.shape, F32)

    dm, xi, ze = dm_ref[0], xi_ref[0], ze_ref[0]
    g_chunk = gch_ref[h]
    for c in (range(n_sub - 1, -1, -1) if reverse else range(n_sub)):
        sl = slice(c * chunk, (c + 1) * chunk)
        q, k, v = q_ref[sl, :], k_ref[sl, :], v_ref[sl, :]
        s = lax.dot_general(q, k, (((1,), (1,)), ((), ())), preferred_element_type=F32) * dm
        st = st_ref[...]
        y = jnp.dot(s.astype(BF16), v, preferred_element_type=F32)
        y = y + jnp.dot((q.astype(F32) * xi).astype(BF16), st.astype(BF16), preferred_element_type=F32)
        kz = (k.astype(F32) * ze).astype(BF16)
        st_ref[...] = g_chunk * st + lax.dot_general(kz, v, (((0,), (0,)), ((), ())),
                                                     preferred_element_type=F32)
        if final:
            gate = gate_ref[sl, :].astype(F32)
            y = _rms(y + yb_ref[sl, :], gain_ref[...]) * (gate * jax.nn.sigmoid(gate))
        o_ref[sl, :] = y.astype(o_ref.dtype)


def _retention_pass(proj, tables, *, seq, heads, q_blk, k_blk, v_blk, gate_blk, reverse, y_bwd=None,
                    gain=None):
    dk, dv, chunk = HEAD_DIM, 2 * HEAD_DIM, RET_CHUNK
    g_chunk, dmask, xi, zeta = tables
    rows = _tile(seq, 8 * chunk)
    n_steps = seq // rows
    final = y_bwd is not None
    blk = (lambda n: n_steps - 1 - n) if reverse else (lambda n: n)
    head_tab = lambda shape: pl.BlockSpec((1,) + shape, lambda h, n: (h, 0, 0))
    in_specs = [pl.BlockSpec(memory_space=pltpu.SMEM),
                pl.BlockSpec((rows, dk), lambda h, n: (blk(n), q_blk + h)),
                pl.BlockSpec((rows, dk), lambda h, n: (blk(n), k_blk + h)),
                pl.BlockSpec((rows, dv), lambda h, n: (blk(n), v_blk + h)),
                head_tab((chunk, chunk)), head_tab((chunk, 1)), head_tab((chunk, 1))]
    args = [g_chunk, proj, proj, proj, dmask, xi, zeta]
    if final:
        in_specs += [pl.BlockSpec((rows, dv), lambda h, n: (blk(n), h)),
                     pl.BlockSpec((rows, dv), lambda h, n: (blk(n), gate_blk + h)),
                     pl.BlockSpec((1, dv), lambda h, n: (0, 0))]
        args += [y_bwd, proj, gain.reshape(1, dv)]
    kern = functools.partial(_ret_kernel, chunk=chunk, n_sub=rows // chunk, reverse=reverse, final=final)
    return pl.pallas_call(
        kern, grid=(heads, n_steps), in_specs=in_specs,
        out_specs=pl.BlockSpec((rows, dv), lambda h, n: (blk(n), h)),
        out_shape=jax.ShapeDtypeStruct((seq, heads * dv), BF16 if final else F32),
        scratch_shapes=[pltpu.VMEM((dk, dv), F32)],
        compiler_params=_cp(2), name="retention")(*args)


def _retention_tables(log_g, reverse):
    c = RET_CHUNK
    idx = jnp.arange(c, dtype=F32)
    diff = idx[:, None] - idx[None, :]
    if reverse:
        diff = -diff
        mask = diff > 0
        xi = jnp.exp((c - idx)[None, :] * log_g[:, None])
        zeta = jnp.exp(idx[None, :] * log_g[:, None])
    else:
        mask = diff >= 0
        xi = jnp.exp((idx + 1.0)[None, :] * log_g[:, None])
        zeta = jnp.exp((c - 1 - idx)[None, :] * log_g[:, None])
    dmask = jnp.where(mask[None], jnp.exp(jnp.where(mask, diff, 0.0)[None] * log_g[:, None, None]), 0.0)
    return jnp.exp(c * log_g), dmask, xi[:, :, None], zeta[:, :, None]


def _dft_tables(seq):
    n = 2 * seq
    n2 = DFT_N2
    n1 = n // n2
    h1 = n1 // 2
    nk1 = n1 // 2 + 1
    k1p = -(-nk1 // 16) * 16
    a2 = np.arange(n2)[:, None, None]
    k1 = np.arange(nk1)[None, :, None]
    a1 = np.arange(h1)[None, None, :]
    theta = 2.0 * np.pi * ((a1 * k1 % n1) / n1 + (a2 * k1 % n) / n)
    fwd = np.zeros((n2, 2 * k1p, h1))
    fwd[:, :nk1] = np.cos(theta)
    fwd[:, k1p:k1p + nk1] = -np.sin(theta)
    weight = np.where((np.arange(nk1) == 0) | (np.arange(nk1) == n1 // 2), 1.0, 2.0)[None, :, None]
    inv = np.zeros((n2, h1, 2 * k1p))
    inv[:, :, :nk1] = np.transpose(weight * np.cos(theta), (0, 2, 1))
    inv[:, :, k1p:k1p + nk1] = np.transpose(-weight * np.sin(theta), (0, 2, 1))
    ang = 2.0 * np.pi * (np.arange(n2)[:, None] * np.arange(n2)[None, :] % n2) / n2
    c, s = np.cos(ang), np.sin(ang)
    f_blk = np.block([[c, s], [-s, c]])
    fi_blk = np.block([[c, -s], [s, c]])
    as_bf16 = lambda a: jnp.asarray(a.astype(BF16))
    return dict(n=n, n1=n1, h1=h1, nk1=nk1, k1p=k1p, fwd=as_bf16(fwd), inv=as_bf16(inv),
                f_blk=as_bf16(f_blk), fi_blk=as_bf16(fi_blk))


def _hy_filter_kernel(feat_ref, w1_ref, b1_ref, w2_ref, b2_ref, w3_ref, fr_ref, dl_ref, h_ref, l1_ref,
                      *, width):
    i = pl.program_id(0)
    feats = feat_ref[...]
    fr = fr_ref[...]
    hid = jnp.sin(fr * (jnp.dot(feats.astype(BF16), w1_ref[...], preferred_element_type=F32) + b1_ref[...]))
    hid = jnp.sin(fr * (jnp.dot(hid.astype(BF16), w2_ref[...], preferred_element_type=F32) + b2_ref[...]))
    h = jnp.dot(hid.astype(BF16), w3_ref[...], preferred_element_type=F32)
    window = jnp.exp(-feats[:, 0:1] * dl_ref[...])
    n_groups = h.shape[1] // width
    row = lax.broadcasted_iota(jnp.int32, (h.shape[0], width), 0)
    not_lag0 = jnp.logical_or(row > 0, i > 0)

    @pl.when(i == 0)
    def _():
        l1_ref[...] = jnp.zeros(l1_ref.shape, F32)

    for g in range(n_groups):
        sl = slice(g * width, (g + 1) * width)
        hg = h[:, sl] * window
        if g >= n_groups // 2:
            hg = jnp.where(not_lag0, hg, 0.0)
        h_ref[:, sl] = hg.astype(h_ref.dtype)
        l1_ref[:, sl] += jnp.sum(jnp.abs(hg), axis=0, keepdims=True)


def _hy_filters(feats, deltas, w1, b1, w2, b2, w3, freq, *, seq, width):
    hidden = w1.shape[1]
    emb = -(-w1.shape[0] // 128) * 128
    feats = jnp.pad(feats, ((0, 0), (0, emb - feats.shape[1])))
    w1 = jnp.pad(w1, ((0, emb - w1.shape[0]), (0, 0)))
    cols = w3.shape[1]
    tl = _tile(seq, 512)
    full = lambda shape: pl.BlockSpec(shape, lambda i: (0,) * len(shape))
    return pl.pallas_call(
        functools.partial(_hy_filter_kernel, width=width), grid=(seq // tl,),
        in_specs=[pl.BlockSpec((tl, emb), lambda i: (i, 0)), full((emb, hidden)), full((1, hidden)),
                  full((hidden, hidden)), full((1, hidden)), full((hidden, cols)), full((1, hidden)),
                  full((1, width))],
        out_specs=[pl.BlockSpec((tl, cols), lambda i: (i, 0)), full((1, cols))],
        out_shape=[jax.ShapeDtypeStruct((seq, cols), BF16), jax.ShapeDtypeStruct((1, cols), F32)],
        compiler_params=_cp(1), name="hy_filters")(
            feats, w1.astype(BF16), b1.reshape(1, hidden), w2.astype(BF16), b2.reshape(1, hidden),
            w3.astype(BF16), freq.reshape(1, hidden), deltas.reshape(1, width))


def _hy_short_conv_kernel(u_ref, up_ref, un_ref, w_ref, b_ref, x0_ref, x1_ref, v_ref, vb_ref):
    i = pl.program_id(0)
    u = u_ref[...]
    tm = u.shape[0]
    row = lax.broadcasted_iota(jnp.int32, u.shape, 0)
    prev_row = jnp.where(i > 0, up_ref[7:8, :], 0.0)
    next_row = jnp.where(i < pl.num_programs(0) - 1, un_ref[0:1, :], 0.0)
    before = jnp.where(row == 0, prev_row, pltpu.roll(u, 1, 0))
    after = jnp.where(row == tm - 1, next_row, pltpu.roll(u, tm - 1, 0))
    y = before * w_ref[0:1, :] + u * w_ref[1:2, :] + after * w_ref[2:3, :] + b_ref[...]
    wd = x0_ref.shape[1]
    x0_ref[...] = y[:, :wd]
    x1_ref[...] = y[:, wd:2 * wd]
    v_ref[...] = y[:, 2 * wd:]
    vb_ref[...] = y[:, 2 * wd:].astype(vb_ref.dtype)


def _hy_short_conv(u, w, b, *, seq, width):
    cols = u.shape[1]
    tm = _tile(seq, 256)
    nb8 = tm // 8
    last8 = seq // 8 - 1
    out = pl.BlockSpec((tm, width), lambda i: (i, 0))
    return pl.pallas_call(
        _hy_short_conv_kernel, grid=(seq // tm,),
        in_specs=[pl.BlockSpec((tm, cols), lambda i: (i, 0)),
                  pl.BlockSpec((8, cols), lambda i: (jnp.maximum(i * nb8 - 1, 0), 0)),
                  pl.BlockSpec((8, cols), lambda i: (jnp.minimum((i + 1) * nb8, last8), 0)),
                  pl.BlockSpec((3, cols), lambda i: (0, 0)), pl.BlockSpec((1, cols), lambda i: (0, 0))],
        out_specs=[out, out, out, out],
        out_shape=[jax.ShapeDtypeStruct((seq, width), F32)] * 3 + [jax.ShapeDtypeStruct((seq, width), BF16)],
        compiler_params=_cp(1), name="hy_short_conv")(u, u, u, w, b.reshape(1, cols))


def _dft_a_kernel(x_ref, g_ref, re_ref, im_ref, *, n_sub, cols, k1p):
    for s in range(n_sub):
        sl = slice(s * cols, (s + 1) * cols)
        t = jnp.dot(g_ref[s], x_ref[:, sl], preferred_element_type=F32)
        re_ref[:, sl] = t[:k1p].astype(re_ref.dtype)
        im_ref[:, sl] = t[k1p:].astype(im_ref.dtype)


def _dft_a(x, tabs, *, cols):
    h1, k1p = tabs["h1"], tabs["k1p"]
    xv = x.reshape(h1, DFT_N2 * cols)
    n_sub = max(1, min(DFT_N2, 8192 // cols))
    spec_o = pl.BlockSpec((k1p, n_sub * cols), lambda i: (0, i))
    out = jax.ShapeDtypeStruct((k1p, DFT_N2 * cols), BF16)
    re, im = pl.pallas_call(
        functools.partial(_dft_a_kernel, n_sub=n_sub, cols=cols, k1p=k1p), grid=(DFT_N2 // n_sub,),
        in_specs=[pl.BlockSpec((h1, n_sub * cols), lambda i: (0, i)),
                  pl.BlockSpec((n_sub, 2 * k1p, h1), lambda i: (i, 0, 0))],
        out_specs=[spec_o, spec_o], out_shape=[out, out],
        compiler_params=_cp(1), name="dft_outer")(xv, tabs["fwd"])
    return re.reshape(k1p, DFT_N2, cols), im.reshape(k1p, DFT_N2, cols)


def _hy_filter_spec_kernel(fr_ref, fi_ref, br_ref, bi_ref, f_ref, w_ref, kr_ref, ki_ref):
    n2 = fr_ref.shape[1]
    f = f_ref[...]
    sf = jnp.dot(f, jnp.concatenate([fr_ref[0], fi_ref[0]], axis=0), preferred_element_type=F32)
    sb = jnp.dot(f, jnp.concatenate([br_ref[0], bi_ref[0]], axis=0), preferred_element_type=F32)
    w = w_ref[...]
    kr_ref[0] = (sf[:n2] + sb[:n2]) * w
    ki_ref[0] = (sf[n2:] - sb[n2:]) * w


def _hy_filter_spectrum(h_re, h_im, tabs, wnorm, *, cols):
    nk1 = tabs["nk1"]
    tc = _tile(cols, 1024)
    nb = cols // tc
    fwd = pl.BlockSpec((1, DFT_N2, tc), lambda k, j: (k, 0, j))
    bwd = pl.BlockSpec((1, DFT_N2, tc), lambda k, j: (k, 0, nb + j))
    out = jax.ShapeDtypeStruct((nk1, DFT_N2, cols), F32)
    return pl.pallas_call(
        _hy_filter_spec_kernel, grid=(nk1, nb),
        in_specs=[fwd, fwd, bwd, bwd, pl.BlockSpec((2 * DFT_N2, 2 * DFT_N2), lambda k, j: (0, 0)),
                  pl.BlockSpec((1, tc), lambda k, j: (0, j))],
        out_specs=[fwd, fwd], out_shape=[out, out],
        compiler_params=_cp(2), name="hy_filter_spectrum")(h_re, h_im, h_re, h_im, tabs["f_blk"], wnorm)


def _hy_conv_kernel(tr_ref, ti_ref, kr_ref, ki_ref, f_ref, fi_ref, cr_ref, ci_ref, *, nk1):
    k1 = pl.program_id(0)
    n2 = tr_ref.shape[1]

    @pl.when(k1 < nk1)
    def _():
        s = jnp.dot(f_ref[...], jnp.concatenate([tr_ref[0], ti_ref[0]], axis=0), preferred_element_type=F32)
        sr, si = s[:n2], s[n2:]
        kr, ki = kr_ref[0], ki_ref[0]
        y = jnp.concatenate([sr * kr - si * ki, sr * ki + si * kr], axis=0).astype(BF16)
        c = jnp.dot(fi_ref[...], y, preferred_element_type=F32)
        cr_ref[0] = c[:n2].astype(cr_ref.dtype)
        ci_ref[0] = c[n2:].astype(ci_ref.dtype)

    @pl.when(k1 >= nk1)
    def _():
        cr_ref[...] = jnp.zeros(cr_ref.shape, cr_ref.dtype)
        ci_ref[...] = jnp.zeros(ci_ref.shape, ci_ref.dtype)


def _hy_conv(t_re, t_im, k_re, k_im, tabs, *, order, cols):
    nk1, k1p = tabs["nk1"], tabs["k1p"]
    tc = _tile(cols, 1024)
    nb = cols // tc
    sig = pl.BlockSpec((1, DFT_N2, tc), lambda k, j: (k, 0, j))
    flt = pl.BlockSpec((1, DFT_N2, tc), lambda k, j: (jnp.minimum(k, nk1 - 1), 0, order * nb + j))
    mat = pl.BlockSpec((2 * DFT_N2, 2 * DFT_N2), lambda k, j: (0, 0))
    out = jax.ShapeDtypeStruct((k1p, DFT_N2, cols), BF16)
    return pl.pallas_call(
        functools.partial(_hy_conv_kernel, nk1=nk1), grid=(k1p, nb),
        in_specs=[sig, sig, flt, flt, mat, mat], out_specs=[sig, sig], out_shape=[out, out],
        compiler_params=_cp(2), name="hy_conv")(t_re, t_im, k_re, k_im, tabs["f_blk"], tabs["fi_blk"])


def _idft_a_kernel(cr_ref, ci_ref, g_ref, x_ref, z_ref, d_ref, o_ref, ob_ref, *, n_sub, cols):
    for s in range(n_sub):
        sl = slice(s * cols, (s + 1) * cols)
        c = jnp.concatenate([cr_ref[:, sl], ci_ref[:, sl]], axis=0)
        y = jnp.dot(g_ref[s], c, preferred_element_type=F32)
        out = x_ref[:, sl] * (y + z_ref[:, sl] * d_ref[...])
        o_ref[:, sl] = out
        ob_ref[:, sl] = out.astype(ob_ref.dtype)


def _idft_a(c_re, c_im, tabs, x_mul, z_prev, d_term, *, seq, cols):
    h1, k1p = tabs["h1"], tabs["k1p"]
    n_sub = max(1, min(DFT_N2, 8192 // cols))
    wide = n_sub * cols
    spec_c = pl.BlockSpec((k1p, wide), lambda i: (0, i))
    spec_x = pl.BlockSpec((h1, wide), lambda i: (0, i))
    view = lambda a: a.reshape(h1, DFT_N2 * cols)
    o, ob = pl.pallas_call(
        functools.partial(_idft_a_kernel, n_sub=n_sub, cols=cols), grid=(DFT_N2 // n_sub,),
        in_specs=[spec_c, spec_c, pl.BlockSpec((n_sub, h1, 2 * k1p), lambda i: (i, 0, 0)), spec_x, spec_x,
                  pl.BlockSpec((1, cols), lambda i: (0, 0))],
        out_specs=[spec_x, spec_x],
        out_shape=[jax.ShapeDtypeStruct((h1, DFT_N2 * cols), F32),
                   jax.ShapeDtypeStruct((h1, DFT_N2 * cols), BF16)],
        compiler_params=_cp(1), name="idft_outer")(
            c_re.reshape(k1p, DFT_N2 * cols), c_im.reshape(k1p, DFT_N2 * cols), tabs["inv"],
            view(x_mul), view(z_prev), d_term.reshape(1, cols))
    return o.reshape(seq, cols), ob.reshape(seq, cols)


def _hyena(hy_u, conv_w, conv_b, filt_params, d_term, feats, deltas, tabs, *, seq, width):
    hwin, l1 = _hy_filters(feats, deltas, *filt_params, seq=seq, width=width)
    n_cols = HY_ORDER * width
    l1 = l1[0, :n_cols] + l1[0, n_cols:]
    wnorm = (1.0 / (tabs["n"] * (l1 + EPS))).reshape(1, n_cols)
    h_re, h_im = _dft_a(hwin, tabs, cols=2 * n_cols)
    k_re, k_im = _hy_filter_spectrum(h_re, h_im, tabs, wnorm, cols=n_cols)
    x0, x1, v, v_bf = _hy_short_conv(hy_u, conv_w, conv_b, seq=seq, width=width)
    z, z_bf = v, v_bf
    for order, x_mul in enumerate((x0, x1)):
        t_re, t_im = _dft_a(z_bf, tabs, cols=width)
        c_re, c_im = _hy_conv(t_re, t_im, k_re, k_im, tabs, order=order, cols=width)
        z, z_bf = _idft_a(c_re, c_im, tabs, x_mul, z, d_term[order], seq=seq, cols=width)
    return z_bf


def _merge_kernel(gl_ref, oa_ref, ob_ref, oc_ref, od_ref, wg_ref, wb_ref, bg_ref, o_ref):
    gl = gl_ref[...]
    acc = None
    for i, o in enumerate((oa_ref, ob_ref, oc_ref, od_ref)):
        gate = jax.nn.sigmoid(jnp.dot(gl, wg_ref[i], preferred_element_type=F32) + bg_ref[i])
        term = gate * jnp.dot(o[...], wb_ref[i], preferred_element_type=F32)
        acc = term if acc is None else acc + term
    o_ref[...] = acc.astype(o_ref.dtype)


def _merge(proj, branches, w_gate, b_gate, w_branch, *, gate_blk, tm=512, tn=1024):
    m, unit = branches[0].shape
    nbr, rank, n = w_gate.shape
    tm, tn = _tile(m, tm), _tile(n, tn)
    br = pl.BlockSpec((tm, unit), lambda j, i: (i, 0))
    return pl.pallas_call(
        _merge_kernel, grid=(n // tn, m // tm),
        in_specs=[pl.BlockSpec((tm, rank), lambda j, i: (i, gate_blk)), br, br, br, br,
                  pl.BlockSpec((nbr, rank, tn), lambda j, i: (0, 0, j)),
                  pl.BlockSpec((nbr, unit, tn), lambda j, i: (0, 0, j)),
                  pl.BlockSpec((nbr, 1, tn), lambda j, i: (0, 0, j))],
        out_specs=pl.BlockSpec((tm, tn), lambda j, i: (i, j)),
        out_shape=jax.ShapeDtypeStruct((m, n), BF16),
        compiler_params=_cp(2), name="merge")(proj, *branches, w_gate.astype(BF16), w_branch.astype(BF16),
                                            b_gate.reshape(nbr, 1, n))


def _xattn_kernel(h_ref, x_ref, wq_ref, kv_ref, wo_ref, gp_ref, gn_ref, xo_ref, ho_ref, *, heads, dh, qscale):
    q = (jnp.dot(h_ref[...], wq_ref[...], preferred_element_type=F32) * qscale).astype(BF16)
    kv = kv_ref[...]
    outs = []
    for hh in range(heads):
        k = kv[:, hh * dh:(hh + 1) * dh]
        v = kv[:, (heads + hh) * dh:(heads + hh + 1) * dh]
        s = lax.dot_general(q[:, hh * dh:(hh + 1) * dh], k, (((1,), (1,)), ((), ())),
                            preferred_element_type=F32)
        p = jnp.exp2(s - jnp.max(s, axis=-1, keepdims=True))
        o = jnp.dot(p.astype(BF16), v, preferred_element_type=F32) / jnp.sum(p, axis=-1, keepdims=True)
        outs.append(o.astype(BF16))
    y = jnp.dot(jnp.concatenate(outs, axis=1), wo_ref[...], preferred_element_type=F32)
    xn = x_ref[...] + _rms(y, gp_ref[...])
    xo_ref[...] = xn
    ho_ref[...] = _rms(xn, gn_ref[...]).astype(ho_ref.dtype)


def _cross_attention(h, x, kv, wq, wo, g_post, g_next):
    m, d = x.shape
    dh, heads = HEAD_DIM, XA_HEADS
    n_mem = kv.shape[0]
    tm = _tile(m, 256)
    row = pl.BlockSpec((tm, d), lambda i: (i, 0))
    vec = pl.BlockSpec((1, d), lambda i: (0, 0))
    full = lambda shape: pl.BlockSpec(shape, lambda i: (0, 0))
    kern = functools.partial(_xattn_kernel, heads=heads, dh=dh, qscale=dh ** -0.5 * LOG2E)
    return pl.pallas_call(
        kern, grid=(m // tm,),
        in_specs=[row, row, full((d, heads * dh)), full((n_mem, 2 * heads * dh)), full((heads * dh, d)),
                  vec, vec],
        out_specs=[row, row],
        out_shape=[jax.ShapeDtypeStruct((m, d), F32), jax.ShapeDtypeStruct((m, d), BF16)],
        compiler_params=_cp(1), name="cross_attention")(
            h, x, wq.astype(BF16), kv, wo.astype(BF16), g_post.reshape(1, d), g_next.reshape(1, d))


def _rope_tables(seq):
    dh = HEAD_DIM
    n_rows = seq // GRID_W
    row = jnp.broadcast_to(jnp.arange(n_rows, dtype=F32)[:, None], (n_rows, GRID_W)).reshape(seq)
    col = jnp.broadcast_to(jnp.arange(GRID_W, dtype=F32)[None, :], (n_rows, GRID_W)).reshape(seq)
    axis_dim = dh // 2
    inv_freq = ROPE_THETA ** (-jnp.arange(0, axis_dim, 2, dtype=F32) / axis_dim)
    ang = jnp.stack([row[:, None] * inv_freq, col[:, None] * inv_freq], axis=1)
    ang = jnp.broadcast_to(ang[:, :, None, :], (seq, 2, 2, axis_dim // 2)).reshape(seq, dh)
    sign = jnp.where((jnp.arange(dh) % (dh // 2)) < dh // 4, -1.0, 1.0).astype(F32)
    return jnp.cos(ang), jnp.sin(ang) * sign[None, :]


def _hyena_tables(seq, width):
    bands = (HY_EMB - 1) // 2
    pos = jnp.arange(seq, dtype=F32)
    t = pos / (seq - 1)
    w = 2.0 * math.pi * pos / seq
    f = jnp.linspace(1e-4, bands - 1, bands, dtype=F32)
    ang = w[:, None] * f[None]
    feats = jnp.concatenate([t[:, None], jnp.cos(ang), -jnp.sin(ang)], axis=-1)
    deltas = jnp.abs(jnp.linspace(math.log(HY_DECAY_TARGET) / HY_SLOW_PCT,
                                  math.log(HY_DECAY_TARGET) / HY_FAST_PCT, width, dtype=F32))
    return feats, deltas


def kernel(x, mem, ffn1_pre_norm, ffn1_w1, ffn1_w3, ffn1_w2, ffn1_post_norm, mix_pre_norm, w_in, diff_lambda, diff_norm, gqa_q_norm, gqa_k_norm, ret_decay_logit, ret_norm, hy_conv_w, hy_conv_b, hy_w1, hy_b1, hy_w2, hy_b2, hy_w3, hy_sin_freq, hy_filter_bias, w_gate_up, b_gate, w_branch, w_out, mix_post_norm, xa_pre_norm, xa_mem_norm, xa_wq, xa_wkv, xa_wo, xa_post_norm, ffn2_pre_norm, ffn2_w1, ffn2_w3, ffn2_w2, ffn2_post_norm):
    batch, seq, d_model = x.shape
    assert batch == 1
    depth = w_in.shape[0]
    dh = HEAD_DIM
    unit = d_model // 4
    da_heads = unit // (2 * dh)
    gqa_heads = unit // dh
    gqa_kv = gqa_heads // 4
    ret_heads = unit // (2 * dh)
    hy_w = HY_ORDER + 1
    names = ("qa", "ka", "va", "qb", "kb", "vb", "qc", "kc", "vc", "gc", "hy", "gate")
    widths = (da_heads * 2 * dh, da_heads * 2 * dh, da_heads * 2 * dh, gqa_heads * dh, gqa_kv * dh,
              gqa_kv * dh, ret_heads * dh, ret_heads * dh, ret_heads * 2 * dh, ret_heads * 2 * dh,
              hy_w * unit, GATE_RANK)
    assert sum(widths) == w_in.shape[2]
    src = dict(zip(names, np.concatenate([[0], np.cumsum(widths)[:-1]]).tolist()))
    wid = dict(zip(names, widths))
    blk_w = dict(qa=2 * dh, ka=2 * dh, va=2 * dh, qb=wid["qb"], kb=wid["kb"], vb=wid["vb"], qc=dh, kc=dh,
                 vc=2 * dh, gc=2 * dh, gate=GATE_RANK)
    order_a = sorted(blk_w, key=lambda nm: -blk_w[nm])
    off, pos = {}, 0
    for nm in order_a:
        assert pos % blk_w[nm] == 0 and wid[nm] % blk_w[nm] == 0, nm
        off[nm] = pos
        pos += wid[nm]
    n_a = pos

    att_scale = dh ** -0.5 * LOG2E
    colscale = jnp.ones((n_a,), F32)
    colscale = colscale.at[off["qa"]:off["qa"] + wid["qa"]].set(att_scale)
    colscale = colscale.at[off["kc"]:off["kc"] + wid["kc"]].set(dh ** -0.5)

    slopes = 2.0 ** (-8.0 * jnp.arange(1, da_heads + 1, dtype=F32) / da_heads)
    rope_cos, rope_sin = _rope_tables(seq)
    hy_feats, hy_deltas = _hyena_tables(seq, unit)
    tabs = _dft_tables(seq)

    xs = x.reshape(seq, d_model)
    mem2 = mem.reshape(mem.shape[1], d_model)
    h = _norm_cast(xs, ffn1_pre_norm[0])
    for l in range(depth):
        xs, h = _ffn(xs, h, ffn1_w1[l], ffn1_w3[l], ffn1_w2[l], ffn1_post_norm[l], mix_pre_norm[l])

        w_l = w_in[l]
        w_a = jnp.concatenate([w_l[:, src[nm]:src[nm] + wid[nm]] for nm in order_a], axis=1).astype(BF16)
        proj = _matmul(h, w_a, BF16, colscale)
        hy_u = _matmul(h, w_l[:, src["hy"]:src["gate"]].astype(BF16), F32)

        lam_init = 0.8 - 0.6 * math.exp(-0.3 * l)
        lp = diff_lambda[l]
        lam = jnp.exp(jnp.sum(lp[0] * lp[1])) - jnp.exp(jnp.sum(lp[2] * lp[3])) + lam_init
        scal = jnp.concatenate([lam.reshape(1), slopes * LOG2E]).astype(F32)
        oa = _flash_attention(proj, proj, proj, seq=seq, heads=da_heads, n_maps=2, q_blk=off["qa"] // (2 * dh),
                              k_blk=off["ka"] // (2 * dh), v_blk=off["va"] // (2 * dh), tq=256,
                              diff_args=(scal, diff_norm[l], 1.0 - lam_init))

        qg, kg, vg = _gqa_prep(proj, rope_cos, rope_sin, gqa_q_norm[l], gqa_k_norm[l], seq=seq,
                               q_w=wid["qb"], kv_w=wid["kb"], q_blk=off["qb"] // wid["qb"],
                               k_blk=off["kb"] // wid["kb"], v_blk=off["vb"] // wid["vb"], qscale=att_scale)
        ob = _flash_attention(qg, kg, vg, seq=seq, heads=gqa_kv, n_maps=gqa_heads // gqa_kv, q_blk=0, k_blk=0,
                              v_blk=0, tq=128)

        log_g = -jax.nn.softplus(-ret_decay_logit[l].astype(F32))
        ret_args = dict(seq=seq, heads=ret_heads, q_blk=off["qc"] // dh, k_blk=off["kc"] // dh,
                        v_blk=off["vc"] // (2 * dh), gate_blk=off["gc"] // (2 * dh))
        y_bwd = _retention_pass(proj, _retention_tables(log_g[1], True), reverse=True, **ret_args)
        oc = _retention_pass(proj, _retention_tables(log_g[0], False), reverse=False, y_bwd=y_bwd,
                             gain=ret_norm[l], **ret_args)

        od = _hyena(hy_u, hy_conv_w[l], hy_conv_b[l],
                    (hy_w1[l], hy_b1[l], hy_w2[l], hy_b2[l], hy_w3[l], hy_sin_freq[l]),
                    hy_filter_bias[l], hy_feats, hy_deltas, tabs, seq=seq, width=unit)

        merged = _merge(proj, (oa, ob, oc, od), w_gate_up[l], b_gate[l], w_branch[l],
                        gate_blk=off["gate"] // GATE_RANK)
        y = _matmul(merged, w_out[l].astype(BF16), F32)
        xs, h = _resid_norm(xs, y, mix_post_norm[l], 1.0, xa_pre_norm[l])

        mem_n = _norm_cast(mem2, xa_mem_norm[l])
        kv = _matmul(mem_n, xa_wkv[l].astype(BF16), BF16)
        xs, h = _cross_attention(h, xs, kv, xa_wq[l], xa_wo[l], xa_post_norm[l], ffn2_pre_norm[l])

        g_next = ffn1_pre_norm[l + 1] if l + 1 < depth else ffn2_pre_norm[l]
        xs, h = _ffn(xs, h, ffn2_w1[l], ffn2_w3[l], ffn2_w2[l], ffn2_post_norm[l], g_next)
    return xs.reshape(batch, seq, d_model)
```

```python
import functools
import math

import numpy as np
import jax
import jax.numpy as jnp
from jax import lax
from jax.experimental import pallas as pl
from jax.experimental.pallas import tpu as pltpu

F32 = jnp.float32
BF16 = jnp.bfloat16

EPS = 1e-6
HEAD_DIM = 128
GATE_RANK = 512
GRID_W = 64
ROPE_THETA = 10000.0
RET_CHUNK = 128
XA_HEADS = 4
HY_ORDER = 2
HY_EMB = 33
HY_DECAY_TARGET = 1e-2
HY_FAST_PCT = 0.3
HY_SLOW_PCT = 1.5
LOG2E = math.log2(math.e)
DFT_N2 = 256
NEG_BIG = -1e30

VMEM_LIMIT_BYTES = 56 * 1024 * 1024


def _cp(n_axes):
    return pltpu.CompilerParams(dimension_semantics=("arbitrary",) * n_axes,
                                vmem_limit_bytes=VMEM_LIMIT_BYTES)


def _tile(n, pref, align=128):
    for t in range(min(n, pref), 0, -1):
        if n % t == 0 and t % align == 0:
            return t
    raise ValueError((n, pref, align))


def _rms(x, gain):
    return x * lax.rsqrt(jnp.mean(x * x, axis=-1, keepdims=True) + EPS) * gain


def _norm_kernel(x_ref, g_ref, h_ref):
    h_ref[...] = _rms(x_ref[...], g_ref[...]).astype(h_ref.dtype)


def _norm_cast(x, gain):
    m, d = x.shape
    tm = _tile(m, 256)
    row = pl.BlockSpec((tm, d), lambda i: (i, 0))
    vec = pl.BlockSpec((1, d), lambda i: (0, 0))
    return pl.pallas_call(
        _norm_kernel, grid=(m // tm,), in_specs=[row, vec], out_specs=row,
        out_shape=jax.ShapeDtypeStruct((m, d), BF16), compiler_params=_cp(1),
        name="norm_cast")(x, gain.reshape(1, d))


def _resid_norm_kernel(x_ref, y_ref, gp_ref, gn_ref, xo_ref, ho_ref, *, alpha):
    xn = x_ref[...] + alpha * _rms(y_ref[...], gp_ref[...])
    xo_ref[...] = xn
    ho_ref[...] = _rms(xn, gn_ref[...]).astype(ho_ref.dtype)


def _resid_norm(x, y, g_post, alpha, g_next):
    m, d = x.shape
    tm = _tile(m, 256)
    row = pl.BlockSpec((tm, d), lambda i: (i, 0))
    vec = pl.BlockSpec((1, d), lambda i: (0, 0))
    return pl.pallas_call(
        functools.partial(_resid_norm_kernel, alpha=alpha), grid=(m // tm,),
        in_specs=[row, row, vec, vec], out_specs=[row, row],
        out_shape=[jax.ShapeDtypeStruct((m, d), F32), jax.ShapeDtypeStruct((m, d), BF16)],
        compiler_params=_cp(1), name="resid_norm")(x, y, g_post.reshape(1, d), g_next.reshape(1, d))


def _mm_kernel(a_ref, w_ref, s_ref, o_ref, wb_ref):
    @pl.when(pl.program_id(1) == 0)
    def _():
        wb_ref[...] = w_ref[...].astype(BF16)

    acc = jnp.dot(a_ref[...], wb_ref[...], preferred_element_type=F32)
    o_ref[...] = (acc * s_ref[...]).astype(o_ref.dtype)


def _matmul(a, w, out_dtype, colscale=None, tm=1024, tn=512):
    m, k = a.shape
    n = w.shape[1]
    tm, tn = _tile(m, tm), _tile(n, tn)
    if colscale is None:
        colscale = jnp.ones((n,), F32)
    return pl.pallas_call(
        _mm_kernel, grid=(n // tn, m // tm),
        in_specs=[pl.BlockSpec((tm, k), lambda j, i: (i, 0)),
                  pl.BlockSpec((k, tn), lambda j, i: (0, j)),
                  pl.BlockSpec((1, tn), lambda j, i: (0, j))],
        out_specs=pl.BlockSpec((tm, tn), lambda j, i: (i, j)),
        out_shape=jax.ShapeDtypeStruct((m, n), out_dtype),
        scratch_shapes=[pltpu.VMEM((k, tn), BF16)],
        compiler_params=_cp(2), name="matmul")(a, w, colscale.reshape(1, n).astype(F32))


def _ffn_up_kernel(h_ref, w1_ref, w3_ref, o_ref, w1b_ref, w3b_ref):
    @pl.when(pl.program_id(1) == 0)
    def _():
        w1b_ref[...] = w1_ref[...].astype(BF16)
        w3b_ref[...] = w3_ref[...].astype(BF16)

    h = h_ref[...]
    a = jnp.dot(h, w1b_ref[...], preferred_element_type=F32)
    b = jnp.dot(h, w3b_ref[...], preferred_element_type=F32)
    o_ref[...] = (a * jax.nn.sigmoid(a) * b).astype(o_ref.dtype)


def _ffn_up(h, w1, w3, tm=1024, tn=256):
    m, k = h.shape
    n = w1.shape[1]
    tm, tn = _tile(m, tm), _tile(n, tn)
    wspec = pl.BlockSpec((k, tn), lambda j, i: (0, j))
    return pl.pallas_call(
        _ffn_up_kernel, grid=(n // tn, m // tm),
        in_specs=[pl.BlockSpec((tm, k), lambda j, i: (i, 0)), wspec, wspec],
        out_specs=pl.BlockSpec((tm, tn), lambda j, i: (i, j)),
        out_shape=jax.ShapeDtypeStruct((m, n), BF16),
        scratch_shapes=[pltpu.VMEM((k, tn), BF16), pltpu.VMEM((k, tn), BF16)],
        compiler_params=_cp(2), name="ffn_up")(h, w1, w3)


def _ffn(x, h, w1, w3, w2, g_post, g_next):
    u = _ffn_up(h, w1, w3)
    y = _matmul(u, w2, F32)
    return _resid_norm(x, y, g_post, 0.5, g_next)


FLASH_SUBSTEPS = 2
FLASH_KEY_BLOCK = 2048


def _alibi_mixed_bases(tq, tk):
    step = math.gcd(tq, tk)
    return [b for b in range(-(tq // step) * step, tk + step, step) if 1 - tq < b < tk - 1], step


def _flash_kernel(*refs, n_maps, dh, tq, tk, nq, nk, diff, post_scale):
    if diff:
        scal_ref, q_ref, k_ref, v_ref, g_ref, o_ref, s_buf, p_buf, al_buf, m_ref, acc_ref, l_ref, tbl_ref = refs
        mixed, base_step = _alibi_mixed_bases(tq, tk)
    else:
        q_ref, k_ref, v_ref, o_ref, s_buf, p_buf, al_buf, m_ref, acc_ref = refs
    h, g = pl.program_id(0), pl.program_id(1)
    n_sub = FLASH_SUBSTEPS
    total = nq * nk

    @pl.when(g == 0)
    def _():
        s_buf[...] = jnp.zeros(s_buf.shape, F32)
        p_buf[...] = jnp.zeros(p_buf.shape, BF16)
        al_buf[...] = jnp.ones(al_buf.shape, F32)
        m_ref[...] = jnp.full(m_ref.shape, NEG_BIG, F32)
        acc_ref[...] = jnp.zeros(acc_ref.shape, F32)
        if diff:
            l_ref[...] = jnp.zeros(l_ref.shape, F32)
            d = lax.broadcasted_iota(jnp.int32, (tq, tk), 0) - lax.broadcasted_iota(jnp.int32, (tq, tk), 1)
            slope = scal_ref[1 + h]
            tbl_ref[0] = d.astype(F32) * slope
            tbl_ref[1] = (-d).astype(F32) * slope
            for i, mb in enumerate(mixed):
                tbl_ref[2 + i] = jnp.abs(d + mb).astype(F32) * slope

    for sub in range(n_sub):
        other = 1 - sub
        t = n_sub * g + sub
        for c in range(n_maps):
            k_lo = c * dh if diff else 0
            s_buf[sub, c] = lax.dot_general(q_ref[:, c * dh:(c + 1) * dh],
                                            k_ref[sub * tk:(sub + 1) * tk, k_lo:k_lo + dh],
                                            (((1,), (1,)), ((), ())), preferred_element_type=F32)
        u = t - 1
        valid = jnp.logical_and(u >= 0, u < total)
        uc = jnp.clip(u, 0, total - 1)
        qb, kb = uc // nk, uc % nk
        first = kb == 0
        par = qb % 2
        if diff:
            base = qb * tq - kb * tk
            below, above = base >= tk - 1, base <= 1 - tq
            sel = jnp.where(below, 0, jnp.where(above, 1, 2 + (base - mixed[0]) // base_step))
            bias = tbl_ref[jnp.clip(sel, 0, 1 + len(mixed))]
            basef = (jnp.zeros((tq, 1), jnp.int32) + base).astype(F32) * scal_ref[1 + h]
            cst = jnp.where(below, -basef, jnp.where(above, basef, 0.0))
        else:
            cst = jnp.zeros((tq, 1), F32)
        cst = jnp.where(valid, cst, -jnp.inf)
        for c in range(n_maps):
            s = s_buf[other, c]
            if diff:
                s = s - bias
            m_prev = jnp.where(first, NEG_BIG, m_ref[par, c])
            m_new = jnp.maximum(m_prev, jnp.max(s, axis=-1, keepdims=True) + cst)
            alpha = jnp.exp2(m_prev - m_new)
            p = jnp.exp2(s - (m_new - cst))
            if diff:
                l_ref[par, c] = alpha * l_ref[par, c] + jnp.sum(p, axis=-1, keepdims=True)
            m_ref[par, c] = m_new
            al_buf[other, c] = alpha
            p_buf[other, c] = p.astype(BF16)
        v = v_ref[sub * tk:(sub + 1) * tk, :]
        for c in range(n_maps):
            acc_ref[c] = al_buf[sub, c] * acc_ref[c] + jnp.dot(p_buf[sub, c], v, preferred_element_type=F32)

    w_last = n_sub * g - 1

    @pl.when(jnp.logical_and(w_last >= 0, w_last % nk == nk - 1))
    def _():
        if diff:
            par_w = (w_last // nk) % 2
            o = acc_ref[0] / l_ref[par_w, 0] - scal_ref[0] * (acc_ref[1] / l_ref[par_w, 1])
            o_ref[...] = (_rms(o, g_ref[...]) * post_scale).astype(o_ref.dtype)
        else:
            for r in range(n_maps):
                a = acc_ref[r]
                o_ref[:, r * dh:(r + 1) * dh] = (a[:, :dh] / a[:, dh:]).astype(o_ref.dtype)


def _flash_attention(q, k, v, *, seq, heads, n_maps, q_blk, k_blk, v_blk, tq, diff_args=None):
    dh = HEAD_DIM
    n_sub = FLASH_SUBSTEPS
    diff = diff_args is not None
    tq = _tile(seq, tq)
    tk = _tile(seq // n_sub, FLASH_KEY_BLOCK)
    nq, nk = seq // tq, seq // tk
    assert nk % n_sub == 0
    pairs = nk // n_sub
    kern = functools.partial(_flash_kernel, n_maps=n_maps, dh=dh, tq=tq, tk=tk, nq=nq, nk=nk, diff=diff,
                             post_scale=diff_args[2] if diff else None)
    in_specs = [pl.BlockSpec((tq, n_maps * dh), lambda h, g: (jnp.minimum(n_sub * g // nk, nq - 1), q_blk + h)),
                pl.BlockSpec((n_sub * tk, n_maps * dh if diff else dh), lambda h, g: (g % pairs, k_blk + h)),
                pl.BlockSpec((n_sub * tk, 2 * dh), lambda h, g: (jnp.maximum(g - 1, 0) % pairs, v_blk + h))]
    args = [q, k, v]
    scratch = [pltpu.VMEM((n_sub, n_maps, tq, tk), F32), pltpu.VMEM((n_sub, n_maps, tq, tk), BF16),
               pltpu.VMEM((n_sub, n_maps, tq, 1), F32), pltpu.VMEM((2, n_maps, tq, 1), F32),
               pltpu.VMEM((n_maps, tq, 2 * dh), F32)]
    if diff:
        scal, gain, _ = diff_args
        in_specs = [pl.BlockSpec(memory_space=pltpu.SMEM)] + in_specs + [pl.BlockSpec((1, 2 * dh), lambda h, g: (0, 0))]
        args = [scal] + args + [gain.reshape(1, 2 * dh)]
        scratch += [pltpu.VMEM((2, n_maps, tq, 1), F32),
                    pltpu.VMEM((2 + len(_alibi_mixed_bases(tq, tk)[0]), tq, tk), F32)]
    return pl.pallas_call(
        kern, grid=(heads, nq * pairs + 1), in_specs=in_specs,
        out_specs=pl.BlockSpec((tq, n_maps * dh), lambda h, g: (jnp.maximum(n_sub * g - n_sub, 0) // nk, h)),
        out_shape=jax.ShapeDtypeStruct((seq, heads * n_maps * dh), BF16),
        scratch_shapes=scratch, compiler_params=_cp(2),
        name="diff_attention" if diff else "gqa_attention")(*args)


def _gqa_prep_kernel(q_ref, k_ref, v_ref, cos_ref, sin_ref, gq_ref, gk_ref, qo_ref, ko_ref, vo_ref, *, qscale):
    cos = cos_ref[...]
    sin = sin_ref[...]
    dh = cos.shape[-1]
    lane = lax.broadcasted_iota(jnp.int32, cos.shape, 1)
    first_half = (lane % (dh // 2)) < (dh // 4)

    def norm_rope(xh, gain):
        xn = _rms(xh.astype(F32), gain)
        rot = jnp.where(first_half, pltpu.roll(xn, dh - dh // 4, 1), pltpu.roll(xn, dh // 4, 1))
        return xn * cos + rot * sin

    for hh in range(q_ref.shape[1] // dh):
        sl = slice(hh * dh, (hh + 1) * dh)
        qo_ref[:, sl] = (norm_rope(q_ref[:, sl], gq_ref[...]) * qscale).astype(qo_ref.dtype)
    for hh in range(k_ref.shape[1] // dh):
        sl = slice(hh * dh, (hh + 1) * dh)
        ko_ref[:, sl] = norm_rope(k_ref[:, sl], gk_ref[...]).astype(ko_ref.dtype)
        vo_ref[:, 2 * hh * dh:(2 * hh + 1) * dh] = v_ref[:, sl]
        vo_ref[:, (2 * hh + 1) * dh:(2 * hh + 2) * dh] = jnp.ones((v_ref.shape[0], dh), vo_ref.dtype)


def _gqa_prep(proj, cos, sin_signed, gq, gk, *, seq, q_w, kv_w, q_blk, k_blk, v_blk, qscale):
    dh = HEAD_DIM
    tm = _tile(seq, 512)
    vec = pl.BlockSpec((1, dh), lambda i: (0, 0))
    tab = pl.BlockSpec((tm, dh), lambda i: (i, 0))
    return pl.pallas_call(
        functools.partial(_gqa_prep_kernel, qscale=qscale), grid=(seq // tm,),
        in_specs=[pl.BlockSpec((tm, q_w), lambda i: (i, q_blk)),
                  pl.BlockSpec((tm, kv_w), lambda i: (i, k_blk)),
                  pl.BlockSpec((tm, kv_w), lambda i: (i, v_blk)), tab, tab, vec, vec],
        out_specs=[pl.BlockSpec((tm, q_w), lambda i: (i, 0)), pl.BlockSpec((tm, kv_w), lambda i: (i, 0)),
                   pl.BlockSpec((tm, 2 * kv_w), lambda i: (i, 0))],
        out_shape=[jax.ShapeDtypeStruct((seq, q_w), BF16), jax.ShapeDtypeStruct((seq, kv_w), BF16),
                   jax.ShapeDtypeStruct((seq, 2 * kv_w), BF16)],
        compiler_params=_cp(1), name="gqa_prep")(proj, proj, proj, cos, sin_signed, gq.reshape(1, dh),
                                                gk.reshape(1, dh))


def _ret_kernel(gch_ref, q_ref, k_ref, v_ref, dm_ref, xi_ref, ze_ref, *rest, chunk, n_sub, reverse, final):
    if final:
        yb_ref, gate_ref, gain_ref, o_ref, st_ref = rest
    else:
        o_ref, st_ref = rest
    h, n = pl.program_id(0), pl.program_id(1)

    @pl.when(n == 0)
    def _():
        st_ref[...] = jnp.zeros(st_ref.shape, F32)

    dm, xi, ze = dm_ref[0], xi_ref[0], ze_ref[0]
    g_chunk = gch_ref[h]
    for c in (range(n_sub - 1, -1, -1) if reverse else range(n_sub)):
        sl = slice(c * chunk, (c + 1) * chunk)
        q, k, v = q_ref[sl, :], k_ref[sl, :], v_ref[sl, :]
        s = lax.dot_general(q, k, (((1,), (1,)), ((), ())), preferred_element_type=F32) * dm
        st = st_ref[...]
        y = jnp.dot(s.astype(BF16), v, preferred_element_type=F32)
        y = y + jnp.dot((q.astype(F32) * xi).astype(BF16), st.astype(BF16), preferred_element_type=F32)
        kz = (k.astype(F32) * ze).astype(BF16)
        st_ref[...] = g_chunk * st + lax.dot_general(kz, v, (((0,), (0,)), ((), ())),
                                                     preferred_element_type=F32)
        if final:
            gate = gate_ref[sl, :].astype(F32)
            y = _rms(y + yb_ref[sl, :], gain_ref[...]) * (gate * jax.nn.sigmoid(gate))
        o_ref[sl, :] = y.astype(o_ref.dtype)


def _retention_pass(proj, tables, *, seq, heads, q_blk, k_blk, v_blk, gate_blk, reverse, y_bwd=None,
                    gain=None):
    dk, dv, chunk = HEAD_DIM, 2 * HEAD_DIM, RET_CHUNK
    g_chunk, dmask, xi, zeta = tables
    rows = _tile(seq, 8 * chunk)
    n_steps = seq // rows
    final = y_bwd is not None
    blk = (lambda n: n_steps - 1 - n) if reverse else (lambda n: n)
    head_tab = lambda shape: pl.BlockSpec((1,) + shape, lambda h, n: (h, 0, 0))
    in_specs = [pl.BlockSpec(memory_space=pltpu.SMEM),
                pl.BlockSpec((rows, dk), lambda h, n: (blk(n), q_blk + h)),
                pl.BlockSpec((rows, dk), lambda h, n: (blk(n), k_blk + h)),
                pl.BlockSpec((rows, dv), lambda h, n: (blk(n), v_blk + h)),
                head_tab((chunk, chunk)), head_tab((chunk, 1)), head_tab((chunk, 1))]
    args = [g_chunk, proj, proj, proj, dmask, xi, zeta]
    if final:
        in_specs += [pl.BlockSpec((rows, dv), lambda h, n: (blk(n), h)),
                     pl.BlockSpec((rows, dv), lambda h, n: (blk(n), gate_blk + h)),
                     pl.BlockSpec((1, dv), lambda h, n: (0, 0))]
        args += [y_bwd, proj, gain.reshape(1, dv)]
    kern = functools.partial(_ret_kernel, chunk=chunk, n_sub=rows // chunk, reverse=reverse, final=final)
    return pl.pallas_call(
        kern, grid=(heads, n_steps), in_specs=in_specs,
        out_specs=pl.BlockSpec((rows, dv), lambda h, n: (blk(n), h)),
        out_shape=jax.ShapeDtypeStruct((seq, heads * dv), BF16 if final else F32),
        scratch_shapes=[pltpu.VMEM((dk, dv), F32)],
        compiler_params=_cp(2), name="retention")(*args)


def _retention_tables(log_g, reverse):
    c = RET_CHUNK
    idx = jnp.arange(c, dtype=F32)
    diff = idx[:, None] - idx[None, :]
    if reverse:
        diff = -diff
        mask = diff > 0
        xi = jnp.exp((c - idx)[None, :] * log_g[:, None])
        zeta = jnp.exp(idx[None, :] * log_g[:, None])
    else:
        mask = diff >= 0
        xi = jnp.exp((idx + 1.0)[None, :] * log_g[:, None])
        zeta = jnp.exp((c - 1 - idx)[None, :] * log_g[:, None])
    dmask = jnp.where(mask[None], jnp.exp(jnp.where(mask, diff, 0.0)[None] * log_g[:, None, None]), 0.0)
    return jnp.exp(c * log_g), dmask, xi[:, :, None], zeta[:, :, None]


def _dft_tables(seq):
    n = 2 * seq
    n2 = DFT_N2
    n1 = n // n2
    h1 = n1 // 2
    nk1 = n1 // 2 + 1
    k1p = -(-nk1 // 16) * 16
    a2 = np.arange(n2)[:, None, None]
    k1 = np.arange(nk1)[None, :, None]
    a1 = np.arange(h1)[None, None, :]
    theta = 2.0 * np.pi * ((a1 * k1 % n1) / n1 + (a2 * k1 % n) / n)
    fwd = np.zeros((n2, 2 * k1p, h1))
    fwd[:, :nk1] = np.cos(theta)
    fwd[:, k1p:k1p + nk1] = -np.sin(theta)
    weight = np.where((np.arange(nk1) == 0) | (np.arange(nk1) == n1 // 2), 1.0, 2.0)[None, :, None]
    inv = np.zeros((n2, h1, 2 * k1p))
    inv[:, :, :nk1] = np.transpose(weight * np.cos(theta), (0, 2, 1))
    inv[:, :, k1p:k1p + nk1] = np.transpose(-weight * np.sin(theta), (0, 2, 1))
    ang = 2.0 * np.pi * (np.arange(n2)[:, None] * np.arange(n2)[None, :] % n2) / n2
    c, s = np.cos(ang), np.sin(ang)
    f_blk = np.block([[c, s], [-s, c]])
    fi_blk = np.block([[c, -s], [s, c]])
    as_bf16 = lambda a: jnp.asarray(a.astype(BF16))
    return dict(n=n, n1=n1, h1=h1, nk1=nk1, k1p=k1p, fwd=as_bf16(fwd), inv=as_bf16(inv),
                f_blk=as_bf16(f_blk), fi_blk=as_bf16(fi_blk))


def _hy_filter_kernel(feat_ref, w1_ref, b1_ref, w2_ref, b2_ref, w3_ref, fr_ref, dl_ref, h_ref, l1_ref,
                      *, width, h1):
    i = pl.program_id(0)
    feats = feat_ref[...]
    fr = fr_ref[...]
    hid = jnp.sin(fr * (jnp.dot(feats.astype(BF16), w1_ref[...], preferred_element_type=F32) + b1_ref[...]))
    hid = jnp.sin(fr * (jnp.dot(hid.astype(BF16), w2_ref[...], preferred_element_type=F32) + b2_ref[...]))
    h = jnp.dot(hid.astype(BF16), w3_ref[...], preferred_element_type=F32)
    window = jnp.exp(-feats[:, 0:1] * dl_ref[...])
    n_groups = h.shape[1] // width
    row = lax.broadcasted_iota(jnp.int32, (h.shape[0], width), 0)
    not_lag0 = jnp.logical_or(row > 0, i > 0)

    @pl.when(i == 0)
    def _():
        l1_ref[...] = jnp.zeros(l1_ref.shape, F32)

    for g in range(n_groups):
        sl = slice(g * width, (g + 1) * width)
        hg = h[:, sl] * window
        if g >= n_groups // 2:
            hg = jnp.where(not_lag0, hg, 0.0)
        l1_ref[:, sl] += jnp.sum(jnp.abs(hg), axis=0, keepdims=True)
        hg = hg.astype(h_ref.dtype)
        for s in range(h.shape[0] // h1):
            lo = s * h.shape[1] + g * width
            h_ref[:, lo:lo + width] = hg[s * h1:(s + 1) * h1, :]


def _hy_filters(feats, deltas, w1, b1, w2, b2, w3, freq, *, seq, width, h1):
    hidden = w1.shape[1]
    emb = -(-w1.shape[0] // 128) * 128
    feats = jnp.pad(feats, ((0, 0), (0, emb - feats.shape[1])))
    feats = feats.reshape(h1, DFT_N2, emb).transpose(1, 0, 2).reshape(seq, emb)
    w1 = jnp.pad(w1, ((0, emb - w1.shape[0]), (0, 0)))
    cols = w3.shape[1]
    tl = max(_tile(seq, 512), h1)
    n_sub = tl // h1
    full = lambda shape: pl.BlockSpec(shape, lambda i: (0,) * len(shape))
    return pl.pallas_call(
        functools.partial(_hy_filter_kernel, width=width, h1=h1), grid=(seq // tl,),
        in_specs=[pl.BlockSpec((tl, emb), lambda i: (i, 0)), full((emb, hidden)), full((1, hidden)),
                  full((hidden, hidden)), full((1, hidden)), full((hidden, cols)), full((1, hidden)),
                  full((1, width))],
        out_specs=[pl.BlockSpec((h1, n_sub * cols), lambda i: (0, i)), full((1, cols))],
        out_shape=[jax.ShapeDtypeStruct((h1, DFT_N2 * cols), BF16), jax.ShapeDtypeStruct((1, cols), F32)],
        compiler_params=_cp(1), name="hy_filters")(
            feats, w1.astype(BF16), b1.reshape(1, hidden), w2.astype(BF16), b2.reshape(1, hidden),
            w3.astype(BF16), freq.reshape(1, hidden), deltas.reshape(1, width))


def _hy_short_conv_kernel(u_ref, up_ref, un_ref, w_ref, b_ref, x0_ref, x1_ref, v_ref, vb_ref):
    i = pl.program_id(0)
    u = u_ref[...]
    tm = u.shape[0]
    row = lax.broadcasted_iota(jnp.int32, u.shape, 0)
    prev_row = jnp.where(i > 0, up_ref[7:8, :], 0.0)
    next_row = jnp.where(i < pl.num_programs(0) - 1, un_ref[0:1, :], 0.0)
    before = jnp.where(row == 0, prev_row, pltpu.roll(u, 1, 0))
    after = jnp.where(row == tm - 1, next_row, pltpu.roll(u, tm - 1, 0))
    y = before * w_ref[0:1, :] + u * w_ref[1:2, :] + after * w_ref[2:3, :] + b_ref[...]
    wd = x0_ref.shape[1]
    x0_ref[...] = y[:, :wd]
    x1_ref[...] = y[:, wd:2 * wd]
    v_ref[...] = y[:, 2 * wd:]
    vb_ref[...] = y[:, 2 * wd:].astype(vb_ref.dtype)


def _hy_short_conv(u, w, b, *, seq, width):
    cols = u.shape[1]
    tm = _tile(seq, 256)
    nb8 = tm // 8
    last8 = seq // 8 - 1
    out = pl.BlockSpec((tm, width), lambda i: (i, 0))
    return pl.pallas_call(
        _hy_short_conv_kernel, grid=(seq // tm,),
        in_specs=[pl.BlockSpec((tm, cols), lambda i: (i, 0)),
                  pl.BlockSpec((8, cols), lambda i: (jnp.maximum(i * nb8 - 1, 0), 0)),
                  pl.BlockSpec((8, cols), lambda i: (jnp.minimum((i + 1) * nb8, last8), 0)),
                  pl.BlockSpec((3, cols), lambda i: (0, 0)), pl.BlockSpec((1, cols), lambda i: (0, 0))],
        out_specs=[out, out, out, out],
        out_shape=[jax.ShapeDtypeStruct((seq, width), F32)] * 3 + [jax.ShapeDtypeStruct((seq, width), BF16)],
        compiler_params=_cp(1), name="hy_short_conv")(u, u, u, w, b.reshape(1, cols))


def _dft_a_kernel(x_ref, g_ref, re_ref, im_ref, *, n_sub, cols, k1p):
    for s in range(n_sub):
        sl = slice(s * cols, (s + 1) * cols)
        t = jnp.dot(g_ref[s], x_ref[:, sl], preferred_element_type=F32)
        re_ref[:, sl] = t[:k1p].astype(re_ref.dtype)
        im_ref[:, sl] = t[k1p:].astype(im_ref.dtype)


def _dft_a(x, tabs, *, cols):
    h1, k1p = tabs["h1"], tabs["k1p"]
    xv = x.reshape(h1, DFT_N2 * cols)
    n_sub = max(1, min(DFT_N2, 8192 // cols))
    spec_o = pl.BlockSpec((k1p, n_sub * cols), lambda i: (0, i))
    out = jax.ShapeDtypeStruct((k1p, DFT_N2 * cols), BF16)
    re, im = pl.pallas_call(
        functools.partial(_dft_a_kernel, n_sub=n_sub, cols=cols, k1p=k1p), grid=(DFT_N2 // n_sub,),
        in_specs=[pl.BlockSpec((h1, n_sub * cols), lambda i: (0, i)),
                  pl.BlockSpec((n_sub, 2 * k1p, h1), lambda i: (i, 0, 0))],
        out_specs=[spec_o, spec_o], out_shape=[out, out],
        compiler_params=_cp(1), name="dft_outer")(xv, tabs["fwd"])
    return re.reshape(k1p, DFT_N2, cols), im.reshape(k1p, DFT_N2, cols)


def _hy_filter_spec_kernel(fr_ref, fi_ref, br_ref, bi_ref, f_ref, w_ref, kr_ref, ki_ref):
    n2 = fr_ref.shape[1]
    f = f_ref[...]
    sf = jnp.dot(f, jnp.concatenate([fr_ref[0], fi_ref[0]], axis=0), preferred_element_type=F32)
    sb = jnp.dot(f, jnp.concatenate([br_ref[0], bi_ref[0]], axis=0), preferred_element_type=F32)
    w = w_ref[...]
    kr_ref[0] = (sf[:n2] + sb[:n2]) * w
    ki_ref[0] = (sf[n2:] - sb[n2:]) * w


def _hy_filter_spectrum(h_re, h_im, tabs, wnorm, *, cols):
    nk1 = tabs["nk1"]
    tc = _tile(cols, 1024)
    nb = cols // tc
    fwd = pl.BlockSpec((1, DFT_N2, tc), lambda k, j: (k, 0, j))
    bwd = pl.BlockSpec((1, DFT_N2, tc), lambda k, j: (k, 0, nb + j))
    out = jax.ShapeDtypeStruct((nk1, DFT_N2, cols), F32)
    return pl.pallas_call(
        _hy_filter_spec_kernel, grid=(nk1, nb),
        in_specs=[fwd, fwd, bwd, bwd, pl.BlockSpec((2 * DFT_N2, 2 * DFT_N2), lambda k, j: (0, 0)),
                  pl.BlockSpec((1, tc), lambda k, j: (0, j))],
        out_specs=[fwd, fwd], out_shape=[out, out],
        compiler_params=_cp(2), name="hy_filter_spectrum")(h_re, h_im, h_re, h_im, tabs["f_blk"], wnorm)


def _hy_conv_kernel(tr_ref, ti_ref, kr_ref, ki_ref, f_ref, fi_ref, cr_ref, ci_ref, *, nk1):
    k1 = pl.program_id(0)
    n2 = tr_ref.shape[1]

    @pl.when(k1 < nk1)
    def _():
        s = jnp.dot(f_ref[...], jnp.concatenate([tr_ref[0], ti_ref[0]], axis=0), preferred_element_type=F32)
        sr, si = s[:n2], s[n2:]
        kr, ki = kr_ref[0], ki_ref[0]
        y = jnp.concatenate([sr * kr - si * ki, sr * ki + si * kr], axis=0).astype(BF16)
        c = jnp.dot(fi_ref[...], y, preferred_element_type=F32)
        cr_ref[0] = c[:n2].astype(cr_ref.dtype)
        ci_ref[0] = c[n2:].astype(ci_ref.dtype)

    @pl.when(k1 >= nk1)
    def _():
        cr_ref[...] = jnp.zeros(cr_ref.shape, cr_ref.dtype)
        ci_ref[...] = jnp.zeros(ci_ref.shape, ci_ref.dtype)


def _hy_conv(t_re, t_im, k_re, k_im, tabs, *, order, cols):
    nk1, k1p = tabs["nk1"], tabs["k1p"]
    tc = _tile(cols, 1024)
    nb = cols // tc
    sig = pl.BlockSpec((1, DFT_N2, tc), lambda k, j: (k, 0, j))
    flt = pl.BlockSpec((1, DFT_N2, tc), lambda k, j: (jnp.minimum(k, nk1 - 1), 0, order * nb + j))
    mat = pl.BlockSpec((2 * DFT_N2, 2 * DFT_N2), lambda k, j: (0, 0))
    out = jax.ShapeDtypeStruct((k1p, DFT_N2, cols), BF16)
    return pl.pallas_call(
        functools.partial(_hy_conv_kernel, nk1=nk1), grid=(k1p, nb),
        in_specs=[sig, sig, flt, flt, mat, mat], out_specs=[sig, sig], out_shape=[out, out],
        compiler_params=_cp(2), name="hy_conv")(t_re, t_im, k_re, k_im, tabs["f_blk"], tabs["fi_blk"])


def _idft_a_kernel(cr_ref, ci_ref, g_ref, x_ref, z_ref, d_ref, o_ref, ob_ref, *, n_sub, cols):
    for s in range(n_sub):
        sl = slice(s * cols, (s + 1) * cols)
        c = jnp.concatenate([cr_ref[:, sl], ci_ref[:, sl]], axis=0)
        y = jnp.dot(g_ref[s], c, preferred_element_type=F32)
        out = x_ref[:, sl] * (y + z_ref[:, sl] * d_ref[...])
        o_ref[:, sl] = out
        ob_ref[:, sl] = out.astype(ob_ref.dtype)


def _idft_a(c_re, c_im, tabs, x_mul, z_prev, d_term, *, seq, cols):
    h1, k1p = tabs["h1"], tabs["k1p"]
    n_sub = max(1, min(DFT_N2, 8192 // cols))
    wide = n_sub * cols
    spec_c = pl.BlockSpec((k1p, wide), lambda i: (0, i))
    spec_x = pl.BlockSpec((h1, wide), lambda i: (0, i))
    view = lambda a: a.reshape(h1, DFT_N2 * cols)
    o, ob = pl.pallas_call(
        functools.partial(_idft_a_kernel, n_sub=n_sub, cols=cols), grid=(DFT_N2 // n_sub,),
        in_specs=[spec_c, spec_c, pl.BlockSpec((n_sub, h1, 2 * k1p), lambda i: (i, 0, 0)), spec_x, spec_x,
                  pl.BlockSpec((1, cols), lambda i: (0, 0))],
        out_specs=[spec_x, spec_x],
        out_shape=[jax.ShapeDtypeStruct((h1, DFT_N2 * cols), F32),
                   jax.ShapeDtypeStruct((h1, DFT_N2 * cols), BF16)],
        compiler_params=_cp(1), name="idft_outer")(
            c_re.reshape(k1p, DFT_N2 * cols), c_im.reshape(k1p, DFT_N2 * cols), tabs["inv"],
            view(x_mul), view(z_prev), d_term.reshape(1, cols))
    return o.reshape(seq, cols), ob.reshape(seq, cols)


def _hyena(hy_u, conv_w, conv_b, filt_params, d_term, feats, deltas, tabs, *, seq, width):
    hwin, l1 = _hy_filters(feats, deltas, *filt_params, seq=seq, width=width, h1=tabs["h1"])
    n_cols = HY_ORDER * width
    l1 = l1[0, :n_cols] + l1[0, n_cols:]
    wnorm = (1.0 / (tabs["n"] * (l1 + EPS))).reshape(1, n_cols)
    h_re, h_im = _dft_a(hwin, tabs, cols=2 * n_cols)
    k_re, k_im = _hy_filter_spectrum(h_re, h_im, tabs, wnorm, cols=n_cols)
    x0, x1, v, v_bf = _hy_short_conv(hy_u, conv_w, conv_b, seq=seq, width=width)
    z, z_bf = v, v_bf
    for order, x_mul in enumerate((x0, x1)):
        t_re, t_im = _dft_a(z_bf, tabs, cols=width)
        c_re, c_im = _hy_conv(t_re, t_im, k_re, k_im, tabs, order=order, cols=width)
        z, z_bf = _idft_a(c_re, c_im, tabs, x_mul, z, d_term[order], seq=seq, cols=width)
    return z_bf


def _merge_kernel(gl_ref, oa_ref, ob_ref, oc_ref, od_ref, wg_ref, wb_ref, bg_ref, o_ref, wgb_ref, wbb_ref):
    @pl.when(pl.program_id(1) == 0)
    def _():
        wgb_ref[...] = wg_ref[...].astype(BF16)
        wbb_ref[...] = wb_ref[...].astype(BF16)

    gl = gl_ref[...]
    acc = None
    for i, o in enumerate((oa_ref, ob_ref, oc_ref, od_ref)):
        gate = jax.nn.sigmoid(jnp.dot(gl, wgb_ref[i], preferred_element_type=F32) + bg_ref[i])
        term = gate * jnp.dot(o[...], wbb_ref[i], preferred_element_type=F32)
        acc = term if acc is None else acc + term
    o_ref[...] = acc.astype(o_ref.dtype)


def _merge(proj, branches, w_gate, b_gate, w_branch, *, gate_blk, tm=512, tn=512):
    m, unit = branches[0].shape
    nbr, rank, n = w_gate.shape
    tm, tn = _tile(m, tm), _tile(n, tn)
    br = pl.BlockSpec((tm, unit), lambda j, i: (i, 0))
    return pl.pallas_call(
        _merge_kernel, grid=(n // tn, m // tm),
        in_specs=[pl.BlockSpec((tm, rank), lambda j, i: (i, gate_blk)), br, br, br, br,
                  pl.BlockSpec((nbr, rank, tn), lambda j, i: (0, 0, j)),
                  pl.BlockSpec((nbr, unit, tn), lambda j, i: (0, 0, j)),
                  pl.BlockSpec((nbr, 1, tn), lambda j, i: (0, 0, j))],
        out_specs=pl.BlockSpec((tm, tn), lambda j, i: (i, j)),
        out_shape=jax.ShapeDtypeStruct((m, n), BF16),
        scratch_shapes=[pltpu.VMEM((nbr, rank, tn), BF16), pltpu.VMEM((nbr, unit, tn), BF16)],
        compiler_params=_cp(2), name="merge")(proj, *branches, w_gate, w_branch, b_gate.reshape(nbr, 1, n))


def _xattn_kernel(h_ref, x_ref, wq_ref, kv_ref, wo_ref, gp_ref, gn_ref, xo_ref, ho_ref, *, heads, dh, qscale):
    q = (jnp.dot(h_ref[...], wq_ref[...], preferred_element_type=F32) * qscale).astype(BF16)
    kv = kv_ref[...]
    outs = []
    for hh in range(heads):
        k = kv[:, hh * dh:(hh + 1) * dh]
        v = kv[:, (heads + hh) * dh:(heads + hh + 1) * dh]
        s = lax.dot_general(q[:, hh * dh:(hh + 1) * dh], k, (((1,), (1,)), ((), ())),
                            preferred_element_type=F32)
        p = jnp.exp2(s - jnp.max(s, axis=-1, keepdims=True))
        o = jnp.dot(p.astype(BF16), v, preferred_element_type=F32) / jnp.sum(p, axis=-1, keepdims=True)
        outs.append(o.astype(BF16))
    y = jnp.dot(jnp.concatenate(outs, axis=1), wo_ref[...], preferred_element_type=F32)
    xn = x_ref[...] + _rms(y, gp_ref[...])
    xo_ref[...] = xn
    ho_ref[...] = _rms(xn, gn_ref[...]).astype(ho_ref.dtype)


def _cross_attention(h, x, kv, wq, wo, g_post, g_next):
    m, d = x.shape
    dh, heads = HEAD_DIM, XA_HEADS
    n_mem = kv.shape[0]
    tm = _tile(m, 256)
    row = pl.BlockSpec((tm, d), lambda i: (i, 0))
    vec = pl.BlockSpec((1, d), lambda i: (0, 0))
    full = lambda shape: pl.BlockSpec(shape, lambda i: (0, 0))
    kern = functools.partial(_xattn_kernel, heads=heads, dh=dh, qscale=dh ** -0.5 * LOG2E)
    return pl.pallas_call(
        kern, grid=(m // tm,),
        in_specs=[row, row, full((d, heads * dh)), full((n_mem, 2 * heads * dh)), full((heads * dh, d)),
                  vec, vec],
        out_specs=[row, row],
        out_shape=[jax.ShapeDtypeStruct((m, d), F32), jax.ShapeDtypeStruct((m, d), BF16)],
        compiler_params=_cp(1), name="cross_attention")(
            h, x, wq.astype(BF16), kv, wo.astype(BF16), g_post.reshape(1, d), g_next.reshape(1, d))


def _rope_tables(seq):
    dh = HEAD_DIM
    n_rows = seq // GRID_W
    row = jnp.broadcast_to(jnp.arange(n_rows, dtype=F32)[:, None], (n_rows, GRID_W)).reshape(seq)
    col = jnp.broadcast_to(jnp.arange(GRID_W, dtype=F32)[None, :], (n_rows, GRID_W)).reshape(seq)
    axis_dim = dh // 2
    inv_freq = ROPE_THETA ** (-jnp.arange(0, axis_dim, 2, dtype=F32) / axis_dim)
    ang = jnp.stack([row[:, None] * inv_freq, col[:, None] * inv_freq], axis=1)
    ang = jnp.broadcast_to(ang[:, :, None, :], (seq, 2, 2, axis_dim // 2)).reshape(seq, dh)
    sign = jnp.where((jnp.arange(dh) % (dh // 2)) < dh // 4, -1.0, 1.0).astype(F32)
    return jnp.cos(ang), jnp.sin(ang) * sign[None, :]


def _hyena_tables(seq, width):
    bands = (HY_EMB - 1) // 2
    pos = jnp.arange(seq, dtype=F32)
    t = pos / (seq - 1)
    w = 2.0 * math.pi * pos / seq
    f = jnp.linspace(1e-4, bands - 1, bands, dtype=F32)
    ang = w[:, None] * f[None]
    feats = jnp.concatenate([t[:, None], jnp.cos(ang), -jnp.sin(ang)], axis=-1)
    deltas = jnp.abs(jnp.linspace(math.log(HY_DECAY_TARGET) / HY_SLOW_PCT,
                                  math.log(HY_DECAY_TARGET) / HY_FAST_PCT, width, dtype=F32))
    return feats, deltas


def kernel(x, mem, ffn1_pre_norm, ffn1_w1, ffn1_w3, ffn1_w2, ffn1_post_norm, mix_pre_norm, w_in, diff_lambda, diff_norm, gqa_q_norm, gqa_k_norm, ret_decay_logit, ret_norm, hy_conv_w, hy_conv_b, hy_w1, hy_b1, hy_w2, hy_b2, hy_w3, hy_sin_freq, hy_filter_bias, w_gate_up, b_gate, w_branch, w_out, mix_post_norm, xa_pre_norm, xa_mem_norm, xa_wq, xa_wkv, xa_wo, xa_post_norm, ffn2_pre_norm, ffn2_w1, ffn2_w3, ffn2_w2, ffn2_post_norm):
    batch, seq, d_model = x.shape
    assert batch == 1
    depth = w_in.shape[0]
    dh = HEAD_DIM
    unit = d_model // 4
    da_heads = unit // (2 * dh)
    gqa_heads = unit // dh
    gqa_kv = gqa_heads // 4
    ret_heads = unit // (2 * dh)
    hy_w = HY_ORDER + 1
    names = ("qa", "ka", "va", "qb", "kb", "vb", "qc", "kc", "vc", "gc", "hy", "gate")
    widths = (da_heads * 2 * dh, da_heads * 2 * dh, da_heads * 2 * dh, gqa_heads * dh, gqa_kv * dh,
              gqa_kv * dh, ret_heads * dh, ret_heads * dh, ret_heads * 2 * dh, ret_heads * 2 * dh,
              hy_w * unit, GATE_RANK)
    assert sum(widths) == w_in.shape[2]
    src = dict(zip(names, np.concatenate([[0], np.cumsum(widths)[:-1]]).tolist()))
    wid = dict(zip(names, widths))
    blk_w = dict(qa=2 * dh, ka=2 * dh, va=2 * dh, qb=wid["qb"], kb=wid["kb"], vb=wid["vb"], qc=dh, kc=dh,
                 vc=2 * dh, gc=2 * dh, gate=GATE_RANK)
    order_a = sorted(blk_w, key=lambda nm: -blk_w[nm])
    off, pos = {}, 0
    for nm in order_a:
        assert pos % blk_w[nm] == 0 and wid[nm] % blk_w[nm] == 0, nm
        off[nm] = pos
        pos += wid[nm]
    n_a = pos

    att_scale = dh ** -0.5 * LOG2E
    colscale = jnp.ones((n_a,), F32)
    colscale = colscale.at[off["qa"]:off["qa"] + wid["qa"]].set(att_scale)
    colscale = colscale.at[off["kc"]:off["kc"] + wid["kc"]].set(dh ** -0.5)

    slopes = 2.0 ** (-8.0 * jnp.arange(1, da_heads + 1, dtype=F32) / da_heads)
    rope_cos, rope_sin = _rope_tables(seq)
    hy_feats, hy_deltas = _hyena_tables(seq, unit)
    tabs = _dft_tables(seq)

    xs = x.reshape(seq, d_model)
    mem2 = mem.reshape(mem.shape[1], d_model)
    h = _norm_cast(xs, ffn1_pre_norm[0])
    for l in range(depth):
        xs, h = _ffn(xs, h, ffn1_w1[l], ffn1_w3[l], ffn1_w2[l], ffn1_post_norm[l], mix_pre_norm[l])

        w_l = w_in[l]
        w_a = jnp.concatenate([w_l[:, src[nm]:src[nm] + wid[nm]] for nm in order_a], axis=1)
        proj = _matmul(h, w_a, BF16, colscale)
        hy_u = _matmul(h, w_l[:, src["hy"]:src["gate"]], F32)

        lam_init = 0.8 - 0.6 * math.exp(-0.3 * l)
        lp = diff_lambda[l]
        lam = jnp.exp(jnp.sum(lp[0] * lp[1])) - jnp.exp(jnp.sum(lp[2] * lp[3])) + lam_init
        scal = jnp.concatenate([lam.reshape(1), slopes * LOG2E]).astype(F32)
        oa = _flash_attention(proj, proj, proj, seq=seq, heads=da_heads, n_maps=2, q_blk=off["qa"] // (2 * dh),
                              k_blk=off["ka"] // (2 * dh), v_blk=off["va"] // (2 * dh), tq=256,
                              diff_args=(scal, diff_norm[l], 1.0 - lam_init))

        qg, kg, vg = _gqa_prep(proj, rope_cos, rope_sin, gqa_q_norm[l], gqa_k_norm[l], seq=seq,
                               q_w=wid["qb"], kv_w=wid["kb"], q_blk=off["qb"] // wid["qb"],
                               k_blk=off["kb"] // wid["kb"], v_blk=off["vb"] // wid["vb"], qscale=att_scale)
        ob = _flash_attention(qg, kg, vg, seq=seq, heads=gqa_kv, n_maps=gqa_heads // gqa_kv, q_blk=0, k_blk=0,
                              v_blk=0, tq=128)

        log_g = -jax.nn.softplus(-ret_decay_logit[l].astype(F32))
        ret_args = dict(seq=seq, heads=ret_heads, q_blk=off["qc"] // dh, k_blk=off["kc"] // dh,
                        v_blk=off["vc"] // (2 * dh), gate_blk=off["gc"] // (2 * dh))
        y_bwd = _retention_pass(proj, _retention_tables(log_g[1], True), reverse=True, **ret_args)
        oc = _retention_pass(proj, _retention_tables(log_g[0], False), reverse=False, y_bwd=y_bwd,
                             gain=ret_norm[l], **ret_args)

        od = _hyena(hy_u, hy_conv_w[l], hy_conv_b[l],
                    (hy_w1[l], hy_b1[l], hy_w2[l], hy_b2[l], hy_w3[l], hy_sin_freq[l]),
                    hy_filter_bias[l], hy_feats, hy_deltas, tabs, seq=seq, width=unit)

        merged = _merge(proj, (oa, ob, oc, od), w_gate_up[l], b_gate[l], w_branch[l],
                        gate_blk=off["gate"] // GATE_RANK)
        y = _matmul(merged, w_out[l], F32)
        xs, h = _resid_norm(xs, y, mix_post_norm[l], 1.0, xa_pre_norm[l])

        mem_n = _norm_cast(mem2, xa_mem_norm[l])
        kv = _matmul(mem_n, xa_wkv[l], BF16)
        xs, h = _cross_attention(h, xs, kv, xa_wq[l], xa_wo[l], xa_post_norm[l], ffn2_pre_norm[l])

        g_next = ffn1_pre_norm[l + 1] if l + 1 < depth else ffn2_pre_norm[l]
        xs, h = _ffn(xs, h, ffn2_w1[l], ffn2_w3[l], ffn2_w2[l], ffn2_post_norm[l], g_next)
    return xs.reshape(batch, seq, d_model)
```

```python
import functools
import math

import numpy as np
import jax
import jax.numpy as jnp
from jax import lax
from jax.experimental import pallas as pl
from jax.experimental.pallas import tpu as pltpu

F32 = jnp.float32
BF16 = jnp.bfloat16

EPS = 1e-6
HEAD_DIM = 128
GATE_RANK = 512
GRID_W = 64
ROPE_THETA = 10000.0
RET_CHUNK = 128
XA_HEADS = 4
HY_ORDER = 2
HY_EMB = 33
HY_DECAY_TARGET = 1e-2
HY_FAST_PCT = 0.3
HY_SLOW_PCT = 1.5
LOG2E = math.log2(math.e)
DFT_N2 = 256
NEG_BIG = -1e30

VMEM_LIMIT_BYTES = 56 * 1024 * 1024


def _cp(n_axes):
    return pltpu.CompilerParams(dimension_semantics=("arbitrary",) * n_axes,
                                vmem_limit_bytes=VMEM_LIMIT_BYTES)


def _tile(n, pref, align=128):
    for t in range(min(n, pref), 0, -1):
        if n % t == 0 and t % align == 0:
            return t
    raise ValueError((n, pref, align))


def _rms(x, gain):
    return x * lax.rsqrt(jnp.mean(x * x, axis=-1, keepdims=True) + EPS) * gain


def _norm_kernel(x_ref, g_ref, h_ref):
    h_ref[...] = _rms(x_ref[...], g_ref[...]).astype(h_ref.dtype)


def _norm_cast(x, gain):
    m, d = x.shape
    tm = _tile(m, 256)
    row = pl.BlockSpec((tm, d), lambda i: (i, 0))
    vec = pl.BlockSpec((1, d), lambda i: (0, 0))
    return pl.pallas_call(
        _norm_kernel, grid=(m // tm,), in_specs=[row, vec], out_specs=row,
        out_shape=jax.ShapeDtypeStruct((m, d), BF16), compiler_params=_cp(1),
        name="norm_cast")(x, gain.reshape(1, d))


def _resid_norm_kernel(x_ref, y_ref, gp_ref, gn_ref, xo_ref, ho_ref, *, alpha):
    xn = x_ref[...] + alpha * _rms(y_ref[...], gp_ref[...])
    xo_ref[...] = xn
    ho_ref[...] = _rms(xn, gn_ref[...]).astype(ho_ref.dtype)


def _resid_norm(x, y, g_post, alpha, g_next):
    m, d = x.shape
    tm = _tile(m, 256)
    row = pl.BlockSpec((tm, d), lambda i: (i, 0))
    vec = pl.BlockSpec((1, d), lambda i: (0, 0))
    return pl.pallas_call(
        functools.partial(_resid_norm_kernel, alpha=alpha), grid=(m // tm,),
        in_specs=[row, row, vec, vec], out_specs=[row, row],
        out_shape=[jax.ShapeDtypeStruct((m, d), F32), jax.ShapeDtypeStruct((m, d), BF16)],
        compiler_params=_cp(1), name="resid_norm")(x, y, g_post.reshape(1, d), g_next.reshape(1, d))


def _mm_kernel(a_ref, w_ref, s_ref, o_ref):
    acc = jnp.dot(a_ref[...], w_ref[...], preferred_element_type=F32)
    o_ref[...] = (acc * s_ref[...]).astype(o_ref.dtype)


def _matmul(a, w, out_dtype, colscale=None, layer=None, tm=512, tn=1024):
    m, k = a.shape
    n = w.shape[-1]
    tm, tn = _tile(m, tm), _tile(n, tn)
    if colscale is None:
        colscale = jnp.ones((n,), F32)
    if layer is None:
        w_spec = pl.BlockSpec((k, tn), lambda j, i: (0, j))
    else:
        w_spec = pl.BlockSpec((None, k, tn), lambda j, i: (layer, 0, j))
    return pl.pallas_call(
        _mm_kernel, grid=(n // tn, m // tm),
        in_specs=[pl.BlockSpec((tm, k), lambda j, i: (i, 0)), w_spec, pl.BlockSpec((1, tn), lambda j, i: (0, j))],
        out_specs=pl.BlockSpec((tm, tn), lambda j, i: (i, j)),
        out_shape=jax.ShapeDtypeStruct((m, n), out_dtype),
        compiler_params=_cp(2), name="matmul")(a, w, colscale.reshape(1, n).astype(F32))


def _ffn_up_kernel(h_ref, w1_ref, w3_ref, o_ref):
    h = h_ref[...]
    a = jnp.dot(h, w1_ref[...], preferred_element_type=F32)
    b = jnp.dot(h, w3_ref[...], preferred_element_type=F32)
    o_ref[...] = (a * jax.nn.sigmoid(a) * b).astype(o_ref.dtype)


def _ffn_up(h, w1, w3, layer, tm=512, tn=512):
    m, k = h.shape
    n = w1.shape[-1]
    tm, tn = _tile(m, tm), _tile(n, tn)
    wspec = pl.BlockSpec((None, k, tn), lambda j, i: (layer, 0, j))
    return pl.pallas_call(
        _ffn_up_kernel, grid=(n // tn, m // tm),
        in_specs=[pl.BlockSpec((tm, k), lambda j, i: (i, 0)), wspec, wspec],
        out_specs=pl.BlockSpec((tm, tn), lambda j, i: (i, j)),
        out_shape=jax.ShapeDtypeStruct((m, n), BF16),
        compiler_params=_cp(2), name="ffn_up")(h, w1, w3)


def _ffn(x, h, w1, w3, w2, layer, g_post, g_next):
    u = _ffn_up(h, w1, w3, layer)
    y = _matmul(u, w2, F32, layer=layer)
    return _resid_norm(x, y, g_post, 0.5, g_next)


FLASH_SUBSTEPS = 2
DIFF_QUERY_BLOCK, DIFF_KEY_BLOCK, GQA_KEY_BLOCK = 256, 1024, 2048
ALIBI_ZERO_LOG2 = 150.0


def _alibi_mixed_bases(tq, tk):
    step = math.gcd(tq, tk)
    return [b for b in range(-(tq // step) * step, tk + step, step) if 1 - tq < b < tk - 1], step


def _flash_kernel(*refs, n_maps, dh, tq, tk, nq, pairs, diff, post_scale):
    if diff:
        (n_ref, qb_ref, pr_ref, fst_ref, lst_ref, scal_ref, q_ref, k_ref, v_ref, g_ref, o_ref,
         s_buf, p_buf, al_buf, m_ref, acc_ref, l_ref, tbl_ref) = refs
        mixed, base_step = _alibi_mixed_bases(tq, tk)
    else:
        q_ref, k_ref, v_ref, o_ref, s_buf, p_buf, al_buf, m_ref, acc_ref = refs
    h, g = pl.program_id(0), pl.program_id(1)
    n_sub = FLASH_SUBSTEPS
    if diff:
        n_real = n_ref[h]

        def entry(e):
            e = jnp.maximum(e, 0)
            return qb_ref[h, e], pr_ref[h, e], fst_ref[h, e] == 1, lst_ref[h, e] == 1
    else:
        n_real = nq * pairs

        def entry(e):
            e = jnp.clip(e, 0, n_real - 1)
            return e // pairs, e % pairs, e % pairs == 0, e % pairs == pairs - 1

    @pl.when(g == 0)
    def _():
        s_buf[...] = jnp.zeros(s_buf.shape, F32)
        p_buf[...] = jnp.zeros(p_buf.shape, BF16)
        al_buf[...] = jnp.ones(al_buf.shape, F32)
        m_ref[...] = jnp.full(m_ref.shape, NEG_BIG, F32)
        acc_ref[...] = jnp.zeros(acc_ref.shape, F32)
        if diff:
            l_ref[...] = jnp.zeros(l_ref.shape, F32)
            d = lax.broadcasted_iota(jnp.int32, (tq, tk), 0) - lax.broadcasted_iota(jnp.int32, (tq, tk), 1)
            slope = scal_ref[1 + h]
            tbl_ref[0] = d.astype(F32) * slope
            tbl_ref[1] = (-d).astype(F32) * slope
            for i, mb in enumerate(mixed):
                tbl_ref[2 + i] = jnp.abs(d + mb).astype(F32) * slope

    def pipeline_step():
        for sub in range(n_sub):
            other = 1 - sub
            for c in range(n_maps):
                k_lo = c * dh if diff else 0
                s_buf[sub, c] = lax.dot_general(q_ref[:, c * dh:(c + 1) * dh],
                                                k_ref[sub * tk:(sub + 1) * tk, k_lo:k_lo + dh],
                                                (((1,), (1,)), ((), ())), preferred_element_type=F32)
            e = g - 1 + sub
            valid = jnp.logical_and(e >= 0, e < n_real)
            qb, pr, fst, _ = entry(e)
            kb = n_sub * pr + other
            par = qb % 2
            if diff:
                base = qb * tq - kb * tk
                below, above = base >= tk - 1, base <= 1 - tq
                sel = jnp.where(below, 0, jnp.where(above, 1, 2 + (base - mixed[0]) // base_step))
                bias = tbl_ref[jnp.clip(sel, 0, 1 + len(mixed))]
                basef = (jnp.zeros((tq, 1), jnp.int32) + base).astype(F32) * scal_ref[1 + h]
                cst = jnp.where(below, -basef, jnp.where(above, basef, 0.0))
            else:
                cst = jnp.zeros((tq, 1), F32)
            cst = jnp.where(valid, cst, -jnp.inf)
            for c in range(n_maps):
                s = s_buf[other, c]
                if diff:
                    s = s - bias
                m_prev = m_ref[par, c]
                if other == 0:
                    m_prev = jnp.where(fst, NEG_BIG, m_prev)
                m_new = jnp.maximum(m_prev, jnp.max(s, axis=-1, keepdims=True) + cst)
                alpha = jnp.exp2(m_prev - m_new)
                p = jnp.exp2(s - (m_new - cst))
                if diff:
                    l_ref[par, c] = alpha * l_ref[par, c] + jnp.sum(p, axis=-1, keepdims=True)
                m_ref[par, c] = m_new
                al_buf[other, c] = alpha
                p_buf[other, c] = p.astype(BF16)
            v = v_ref[sub * tk:(sub + 1) * tk, :]
            for c in range(n_maps):
                acc_ref[c] = al_buf[sub, c] * acc_ref[c] + jnp.dot(p_buf[sub, c], v, preferred_element_type=F32)

    if diff:
        pl.when(g <= n_real)(pipeline_step)
    else:
        pipeline_step()

    qb_w, _, _, last_w = entry(g - 1)

    @pl.when(jnp.logical_and(jnp.logical_and(g >= 1, g <= n_real), last_w))
    def _():
        if diff:
            par_w = qb_w % 2
            o = acc_ref[0] / l_ref[par_w, 0] - scal_ref[0] * (acc_ref[1] / l_ref[par_w, 1])
            o_ref[...] = (_rms(o, g_ref[...]) * post_scale).astype(o_ref.dtype)
        else:
            for r in range(n_maps):
                a = acc_ref[r]
                o_ref[:, r * dh:(r + 1) * dh] = (a[:, :dh] / a[:, dh:]).astype(o_ref.dtype)


def _alibi_schedule(reach, *, seq, tq, span):
    nq, pairs = seq // tq, seq // span
    q_lo = jnp.arange(nq, dtype=jnp.int32) * tq
    lo = jnp.clip((q_lo[None, :] - reach[:, None]) // span, 0, pairs - 1)
    hi = jnp.clip((q_lo[None, :] + (tq - 1) + reach[:, None]) // span, 0, pairs - 1)
    cnt = hi - lo + 1
    ends = jnp.cumsum(cnt, axis=1)
    starts = ends - cnt
    n_real = ends[:, -1]
    e = jnp.minimum(jnp.arange(nq * pairs + 1, dtype=jnp.int32)[None, :], n_real[:, None] - 1)
    qb = jax.vmap(lambda en, x: jnp.searchsorted(en, x, side="right"))(ends, e).astype(jnp.int32)
    st = jnp.take_along_axis(starts, qb, axis=1)
    pr = jnp.take_along_axis(lo, qb, axis=1) + e - st
    fst = (e == st).astype(jnp.int32)
    lst = (e == jnp.take_along_axis(ends, qb, axis=1) - 1).astype(jnp.int32)
    return n_real.astype(jnp.int32), qb, pr.astype(jnp.int32), fst, lst


def _flash_attention(q, k, v, *, seq, heads, n_maps, q_blk, k_blk, v_blk, tq, tk, diff_args=None):
    dh = HEAD_DIM
    n_sub = FLASH_SUBSTEPS
    diff = diff_args is not None
    tq = _tile(seq, tq)
    tk = _tile(seq // n_sub, tk)
    nq, pairs = seq // tq, seq // (n_sub * tk)
    kern = functools.partial(_flash_kernel, n_maps=n_maps, dh=dh, tq=tq, tk=tk, nq=nq, pairs=pairs, diff=diff,
                             post_scale=diff_args[2] if diff else None)
    scratch = [pltpu.VMEM((n_sub, n_maps, tq, tk), F32), pltpu.VMEM((n_sub, n_maps, tq, tk), BF16),
               pltpu.VMEM((n_sub, n_maps, tq, 1), F32), pltpu.VMEM((2, n_maps, tq, 1), F32),
               pltpu.VMEM((n_maps, tq, 2 * dh), F32)]
    out_shape = jax.ShapeDtypeStruct((seq, heads * n_maps * dh), BF16)
    if not diff:
        last = nq * pairs - 1
        return pl.pallas_call(
            kern, grid=(heads, nq * pairs + 1),
            in_specs=[pl.BlockSpec((tq, n_maps * dh), lambda h, g: (jnp.minimum(g, last) // pairs, q_blk + h)),
                      pl.BlockSpec((n_sub * tk, dh), lambda h, g: (jnp.minimum(g, last) % pairs, k_blk + h)),
                      pl.BlockSpec((n_sub * tk, 2 * dh), lambda h, g: (jnp.maximum(g - 1, 0) % pairs, v_blk + h))],
            out_specs=pl.BlockSpec((tq, n_maps * dh), lambda h, g: (jnp.maximum(g - 1, 0) // pairs, h)),
            out_shape=out_shape, scratch_shapes=scratch, compiler_params=_cp(2), name="gqa_attention")(q, k, v)
    scal, gain, _, schedule = diff_args
    scratch += [pltpu.VMEM((2, n_maps, tq, 1), F32),
                pltpu.VMEM((2 + len(_alibi_mixed_bases(tq, tk)[0]), tq, tk), F32)]
    prev = lambda g: jnp.maximum(g - 1, 0)
    grid_spec = pltpu.PrefetchScalarGridSpec(
        num_scalar_prefetch=5, grid=(heads, nq * pairs + 1),
        in_specs=[pl.BlockSpec(memory_space=pltpu.SMEM),
                  pl.BlockSpec((tq, n_maps * dh), lambda h, g, n, qb, pr, fs, ls: (qb[h, g], q_blk + h)),
                  pl.BlockSpec((n_sub * tk, n_maps * dh), lambda h, g, n, qb, pr, fs, ls: (pr[h, g], k_blk + h)),
                  pl.BlockSpec((n_sub * tk, 2 * dh), lambda h, g, n, qb, pr, fs, ls: (pr[h, prev(g)], v_blk + h)),
                  pl.BlockSpec((1, 2 * dh), lambda h, g, n, qb, pr, fs, ls: (0, 0))],
        out_specs=pl.BlockSpec((tq, n_maps * dh), lambda h, g, n, qb, pr, fs, ls: (qb[h, prev(g)], h)),
        scratch_shapes=scratch)
    return pl.pallas_call(kern, grid_spec=grid_spec, out_shape=out_shape, compiler_params=_cp(2),
                          name="diff_attention")(*schedule, scal, q, k, v, gain.reshape(1, 2 * dh))


def _norm_max_kernel(x_ref, o_ref, *, dh):
    @pl.when(pl.program_id(1) == 0)
    def _():
        o_ref[...] = jnp.zeros(o_ref.shape, F32)

    x = x_ref[...].astype(F32)
    sq = x * x
    for c in range(x.shape[1] // dh):
        top = jnp.max(jnp.sum(sq[:, c * dh:(c + 1) * dh], axis=1, keepdims=True), axis=0, keepdims=True)
        o_ref[0, :, c * dh:(c + 1) * dh] = jnp.maximum(o_ref[0, :, c * dh:(c + 1) * dh], top)


def _head_norm_max(proj, *, seq, first_blk, n_blk):
    dh = HEAD_DIM
    tm = _tile(seq, 1024)
    out = pl.pallas_call(
        functools.partial(_norm_max_kernel, dh=dh), grid=(n_blk, seq // tm),
        in_specs=[pl.BlockSpec((tm, 2 * dh), lambda b, i: (i, first_blk + b))],
        out_specs=pl.BlockSpec((1, 8, 2 * dh), lambda b, i: (b, 0, 0)),
        out_shape=jax.ShapeDtypeStruct((n_blk, 8, 2 * dh), F32),
        compiler_params=_cp(2), name="head_norm_max")(proj)
    return out[:, 0, ::dh]


def _gqa_prep_kernel(q_ref, k_ref, v_ref, cos_ref, sin_ref, gq_ref, gk_ref, qo_ref, ko_ref, vo_ref, *, qscale):
    cos = cos_ref[...]
    sin = sin_ref[...]
    dh = cos.shape[-1]
    lane = lax.broadcasted_iota(jnp.int32, cos.shape, 1)
    first_half = (lane % (dh // 2)) < (dh // 4)

    def norm_rope(xh, gain):
        xn = _rms(xh.astype(F32), gain)
        rot = jnp.where(first_half, pltpu.roll(xn, dh - dh // 4, 1), pltpu.roll(xn, dh // 4, 1))
        return xn * cos + rot * sin

    for hh in range(q_ref.shape[1] // dh):
        sl = slice(hh * dh, (hh + 1) * dh)
        qo_ref[:, sl] = (norm_rope(q_ref[:, sl], gq_ref[...]) * qscale).astype(qo_ref.dtype)
    for hh in range(k_ref.shape[1] // dh):
        sl = slice(hh * dh, (hh + 1) * dh)
        ko_ref[:, sl] = norm_rope(k_ref[:, sl], gk_ref[...]).astype(ko_ref.dtype)
        vo_ref[:, 2 * hh * dh:(2 * hh + 1) * dh] = v_ref[:, sl]
        vo_ref[:, (2 * hh + 1) * dh:(2 * hh + 2) * dh] = jnp.ones((v_ref.shape[0], dh), vo_ref.dtype)


def _gqa_prep(proj, cos, sin_signed, gq, gk, *, seq, q_w, kv_w, q_blk, k_blk, v_blk, qscale):
    dh = HEAD_DIM
    tm = _tile(seq, 512)
    vec = pl.BlockSpec((1, dh), lambda i: (0, 0))
    tab = pl.BlockSpec((tm, dh), lambda i: (i, 0))
    return pl.pallas_call(
        functools.partial(_gqa_prep_kernel, qscale=qscale), grid=(seq // tm,),
        in_specs=[pl.BlockSpec((tm, q_w), lambda i: (i, q_blk)),
                  pl.BlockSpec((tm, kv_w), lambda i: (i, k_blk)),
                  pl.BlockSpec((tm, kv_w), lambda i: (i, v_blk)), tab, tab, vec, vec],
        out_specs=[pl.BlockSpec((tm, q_w), lambda i: (i, 0)), pl.BlockSpec((tm, kv_w), lambda i: (i, 0)),
                   pl.BlockSpec((tm, 2 * kv_w), lambda i: (i, 0))],
        out_shape=[jax.ShapeDtypeStruct((seq, q_w), BF16), jax.ShapeDtypeStruct((seq, kv_w), BF16),
                   jax.ShapeDtypeStruct((seq, 2 * kv_w), BF16)],
        compiler_params=_cp(1), name="gqa_prep")(proj, proj, proj, cos, sin_signed, gq.reshape(1, dh),
                                                gk.reshape(1, dh))


def _ret_kernel(gch_ref, q_ref, k_ref, v_ref, dm_ref, xi_ref, ze_ref, *rest, chunk, n_sub, reverse, final):
    if final:
        yb_ref, gate_ref, gain_ref, o_ref, st_ref = rest
    else:
        o_ref, st_ref = rest
    h, n = pl.program_id(0), pl.program_id(1)

    @pl.when(n == 0)
    def _():
        st_ref[...] = jnp.zeros(st_ref.shape, F32)

    dm, xi, ze = dm_ref[0], xi_ref[0], ze_ref[0]
    g_chunk = gch_ref[h]
    for c in (range(n_sub - 1, -1, -1) if reverse else range(n_sub)):
        sl = slice(c * chunk, (c + 1) * chunk)
        q, k, v = q_ref[sl, :], k_ref[sl, :], v_ref[sl, :]
        s = lax.dot_general(q, k, (((1,), (1,)), ((), ())), preferred_element_type=F32) * dm
        st = st_ref[...]
        y = jnp.dot(s.astype(BF16), v, preferred_element_type=F32)
        y = y + jnp.dot((q.astype(F32) * xi).astype(BF16), st.astype(BF16), preferred_element_type=F32)
        kz = (k.astype(F32) * ze).astype(BF16)
        st_ref[...] = g_chunk * st + lax.dot_general(kz, v, (((0,), (0,)), ((), ())),
                                                     preferred_element_type=F32)
        if final:
            gate = gate_ref[sl, :].astype(F32)
            y = _rms(y + yb_ref[sl, :], gain_ref[...]) * (gate * jax.nn.sigmoid(gate))
        o_ref[sl, :] = y.astype(o_ref.dtype)


def _retention_pass(proj, tables, *, seq, heads, q_blk, k_blk, v_blk, gate_blk, reverse, y_bwd=None,
                    gain=None):
    dk, dv, chunk = HEAD_DIM, 2 * HEAD_DIM, RET_CHUNK
    g_chunk, dmask, xi, zeta = tables
    rows = _tile(seq, 8 * chunk)
    n_steps = seq // rows
    final = y_bwd is not None
    blk = (lambda n: n_steps - 1 - n) if reverse else (lambda n: n)
    head_tab = lambda shape: pl.BlockSpec((1,) + shape, lambda h, n: (h, 0, 0))
    in_specs = [pl.BlockSpec(memory_space=pltpu.SMEM),
                pl.BlockSpec((rows, dk), lambda h, n: (blk(n), q_blk + h)),
                pl.BlockSpec((rows, dk), lambda h, n: (blk(n), k_blk + h)),
                pl.BlockSpec((rows, dv), lambda h, n: (blk(n), v_blk + h)),
                head_tab((chunk, chunk)), head_tab((chunk, 1)), head_tab((chunk, 1))]
    args = [g_chunk, proj, proj, proj, dmask, xi, zeta]
    if final:
        in_specs += [pl.BlockSpec((rows, dv), lambda h, n: (blk(n), h)),
                     pl.BlockSpec((rows, dv), lambda h, n: (blk(n), gate_blk + h)),
                     pl.BlockSpec((1, dv), lambda h, n: (0, 0))]
        args += [y_bwd, proj, gain.reshape(1, dv)]
    kern = functools.partial(_ret_kernel, chunk=chunk, n_sub=rows // chunk, reverse=reverse, final=final)
    return pl.pallas_call(
        kern, grid=(heads, n_steps), in_specs=in_specs,
        out_specs=pl.BlockSpec((rows, dv), lambda h, n: (blk(n), h)),
        out_shape=jax.ShapeDtypeStruct((seq, heads * dv), BF16 if final else F32),
        scratch_shapes=[pltpu.VMEM((dk, dv), F32)],
        compiler_params=_cp(2), name="retention")(*args)


def _retention_tables(log_g, reverse):
    c = RET_CHUNK
    idx = jnp.arange(c, dtype=F32)
    diff = idx[:, None] - idx[None, :]
    if reverse:
        diff = -diff
        mask = diff > 0
        xi = jnp.exp((c - idx)[None, :] * log_g[:, None])
        zeta = jnp.exp(idx[None, :] * log_g[:, None])
    else:
        mask = diff >= 0
        xi = jnp.exp((idx + 1.0)[None, :] * log_g[:, None])
        zeta = jnp.exp((c - 1 - idx)[None, :] * log_g[:, None])
    dmask = jnp.where(mask[None], jnp.exp(jnp.where(mask, diff, 0.0)[None] * log_g[:, None, None]), 0.0)
    return jnp.exp(c * log_g), dmask, xi[:, :, None], zeta[:, :, None]


def _dft_tables(seq):
    n = 2 * seq
    n2 = DFT_N2
    n1 = n // n2
    h1 = n1 // 2
    nk1 = n1 // 2 + 1
    k1p = -(-nk1 // 16) * 16
    a2 = np.arange(n2)[:, None, None]
    k1 = np.arange(nk1)[None, :, None]
    a1 = np.arange(h1)[None, None, :]
    theta = 2.0 * np.pi * ((a1 * k1 % n1) / n1 + (a2 * k1 % n) / n)
    fwd = np.zeros((n2, 2 * k1p, h1))
    fwd[:, :nk1] = np.cos(theta)
    fwd[:, k1p:k1p + nk1] = -np.sin(theta)
    weight = np.where((np.arange(nk1) == 0) | (np.arange(nk1) == n1 // 2), 1.0, 2.0)[None, :, None]
    inv = np.zeros((n2, h1, 2 * k1p))
    inv[:, :, :nk1] = np.transpose(weight * np.cos(theta), (0, 2, 1))
    inv[:, :, k1p:k1p + nk1] = np.transpose(-weight * np.sin(theta), (0, 2, 1))
    ang = 2.0 * np.pi * (np.arange(n2)[:, None] * np.arange(n2)[None, :] % n2) / n2
    c, s = np.cos(ang), np.sin(ang)
    f_blk = np.block([[c, s], [-s, c]])
    fi_blk = np.block([[c, -s], [s, c]])
    as_bf16 = lambda a: jnp.asarray(a.astype(BF16))
    return dict(n=n, n1=n1, h1=h1, nk1=nk1, k1p=k1p, fwd=as_bf16(fwd), inv=as_bf16(inv),
                f_blk=as_bf16(f_blk), fi_blk=as_bf16(fi_blk))


def _hy_filter_kernel(feat_ref, w1_ref, b1_ref, w2_ref, b2_ref, w3_ref, fr_ref, dl_ref, h_ref, l1_ref,
                      *, width, h1):
    i = pl.program_id(0)
    feats = feat_ref[...]
    fr = fr_ref[...]
    hid = jnp.sin(fr * (jnp.dot(feats.astype(BF16), w1_ref[...], preferred_element_type=F32) + b1_ref[...]))
    hid = jnp.sin(fr * (jnp.dot(hid.astype(BF16), w2_ref[...], preferred_element_type=F32) + b2_ref[...]))
    h = jnp.dot(hid.astype(BF16), w3_ref[...], preferred_element_type=F32)
    window = jnp.exp(-feats[:, 0:1] * dl_ref[...])
    n_groups = h.shape[1] // width
    row = lax.broadcasted_iota(jnp.int32, (h.shape[0], width), 0)
    not_lag0 = jnp.logical_or(row > 0, i > 0)

    @pl.when(i == 0)
    def _():
        l1_ref[...] = jnp.zeros(l1_ref.shape, F32)

    for g in range(n_groups):
        sl = slice(g * width, (g + 1) * width)
        hg = h[:, sl] * window
        if g >= n_groups // 2:
            hg = jnp.where(not_lag0, hg, 0.0)
        l1_ref[:, sl] += jnp.sum(jnp.abs(hg), axis=0, keepdims=True)
        hg = hg.astype(h_ref.dtype)
        for s in range(h.shape[0] // h1):
            lo = s * h.shape[1] + g * width
            h_ref[:, lo:lo + width] = hg[s * h1:(s + 1) * h1, :]


def _hy_filters(feats, deltas, w1, b1, w2, b2, w3, freq, *, seq, width, h1):
    hidden = w1.shape[1]
    emb = -(-w1.shape[0] // 128) * 128
    feats = jnp.pad(feats, ((0, 0), (0, emb - feats.shape[1])))
    feats = feats.reshape(h1, DFT_N2, emb).transpose(1, 0, 2).reshape(seq, emb)
    w1 = jnp.pad(w1, ((0, emb - w1.shape[0]), (0, 0)))
    cols = w3.shape[1]
    tl = max(_tile(seq, 512), h1)
    n_sub = tl // h1
    full = lambda shape: pl.BlockSpec(shape, lambda i: (0,) * len(shape))
    return pl.pallas_call(
        functools.partial(_hy_filter_kernel, width=width, h1=h1), grid=(seq // tl,),
        in_specs=[pl.BlockSpec((tl, emb), lambda i: (i, 0)), full((emb, hidden)), full((1, hidden)),
                  full((hidden, hidden)), full((1, hidden)), full((hidden, cols)), full((1, hidden)),
                  full((1, width))],
        out_specs=[pl.BlockSpec((h1, n_sub * cols), lambda i: (0, i)), full((1, cols))],
        out_shape=[jax.ShapeDtypeStruct((h1, DFT_N2 * cols), BF16), jax.ShapeDtypeStruct((1, cols), F32)],
        compiler_params=_cp(1), name="hy_filters")(
            feats, w1.astype(BF16), b1.reshape(1, hidden), w2.astype(BF16), b2.reshape(1, hidden),
            w3.astype(BF16), freq.reshape(1, hidden), deltas.reshape(1, width))


def _hy_short_conv_kernel(u_ref, up_ref, un_ref, w_ref, b_ref, x0_ref, x1_ref, v_ref, vb_ref):
    i = pl.program_id(0)
    u = u_ref[...]
    tm = u.shape[0]
    row = lax.broadcasted_iota(jnp.int32, u.shape, 0)
    prev_row = jnp.where(i > 0, up_ref[7:8, :], 0.0)
    next_row = jnp.where(i < pl.num_programs(0) - 1, un_ref[0:1, :], 0.0)
    before = jnp.where(row == 0, prev_row, pltpu.roll(u, 1, 0))
    after = jnp.where(row == tm - 1, next_row, pltpu.roll(u, tm - 1, 0))
    y = before * w_ref[0:1, :] + u * w_ref[1:2, :] + after * w_ref[2:3, :] + b_ref[...]
    wd = x0_ref.shape[1]
    x0_ref[...] = y[:, :wd]
    x1_ref[...] = y[:, wd:2 * wd]
    v_ref[...] = y[:, 2 * wd:]
    vb_ref[...] = y[:, 2 * wd:].astype(vb_ref.dtype)


def _hy_short_conv(u, w, b, *, seq, width):
    cols = u.shape[1]
    tm = _tile(seq, 256)
    nb8 = tm // 8
    last8 = seq // 8 - 1
    out = pl.BlockSpec((tm, width), lambda i: (i, 0))
    return pl.pallas_call(
        _hy_short_conv_kernel, grid=(seq // tm,),
        in_specs=[pl.BlockSpec((tm, cols), lambda i: (i, 0)),
                  pl.BlockSpec((8, cols), lambda i: (jnp.maximum(i * nb8 - 1, 0), 0)),
                  pl.BlockSpec((8, cols), lambda i: (jnp.minimum((i + 1) * nb8, last8), 0)),
                  pl.BlockSpec((3, cols), lambda i: (0, 0)), pl.BlockSpec((1, cols), lambda i: (0, 0))],
        out_specs=[out, out, out, out],
        out_shape=[jax.ShapeDtypeStruct((seq, width), F32)] * 3 + [jax.ShapeDtypeStruct((seq, width), BF16)],
        compiler_params=_cp(1), name="hy_short_conv")(u, u, u, w, b.reshape(1, cols))


def _dft_a_kernel(x_ref, g_ref, re_ref, im_ref, *, n_sub, cols, k1p):
    for s in range(n_sub):
        sl = slice(s * cols, (s + 1) * cols)
        t = jnp.dot(g_ref[s], x_ref[:, sl], preferred_element_type=F32)
        re_ref[:, sl] = t[:k1p].astype(re_ref.dtype)
        im_ref[:, sl] = t[k1p:].astype(im_ref.dtype)


def _dft_a(x, tabs, *, cols):
    h1, k1p = tabs["h1"], tabs["k1p"]
    xv = x.reshape(h1, DFT_N2 * cols)
    n_sub = max(1, min(DFT_N2, 8192 // cols))
    spec_o = pl.BlockSpec((k1p, n_sub * cols), lambda i: (0, i))
    out = jax.ShapeDtypeStruct((k1p, DFT_N2 * cols), BF16)
    re, im = pl.pallas_call(
        functools.partial(_dft_a_kernel, n_sub=n_sub, cols=cols, k1p=k1p), grid=(DFT_N2 // n_sub,),
        in_specs=[pl.BlockSpec((h1, n_sub * cols), lambda i: (0, i)),
                  pl.BlockSpec((n_sub, 2 * k1p, h1), lambda i: (i, 0, 0))],
        out_specs=[spec_o, spec_o], out_shape=[out, out],
        compiler_params=_cp(1), name="dft_outer")(xv, tabs["fwd"])
    return re.reshape(k1p, DFT_N2, cols), im.reshape(k1p, DFT_N2, cols)


def _hy_filter_spec_kernel(fr_ref, fi_ref, br_ref, bi_ref, f_ref, w_ref, kr_ref, ki_ref):
    n2 = fr_ref.shape[1]
    f = f_ref[...]
    sf = jnp.dot(f, jnp.concatenate([fr_ref[0], fi_ref[0]], axis=0), preferred_element_type=F32)
    sb = jnp.dot(f, jnp.concatenate([br_ref[0], bi_ref[0]], axis=0), preferred_element_type=F32)
    w = w_ref[...]
    kr_ref[0] = (sf[:n2] + sb[:n2]) * w
    ki_ref[0] = (sf[n2:] - sb[n2:]) * w


def _hy_filter_spectrum(h_re, h_im, tabs, wnorm, *, cols):
    nk1 = tabs["nk1"]
    tc = _tile(cols, 1024)
    nb = cols // tc
    fwd = pl.BlockSpec((1, DFT_N2, tc), lambda k, j: (k, 0, j))
    bwd = pl.BlockSpec((1, DFT_N2, tc), lambda k, j: (k, 0, nb + j))
    out = jax.ShapeDtypeStruct((nk1, DFT_N2, cols), F32)
    return pl.pallas_call(
        _hy_filter_spec_kernel, grid=(nk1, nb),
        in_specs=[fwd, fwd, bwd, bwd, pl.BlockSpec((2 * DFT_N2, 2 * DFT_N2), lambda k, j: (0, 0)),
                  pl.BlockSpec((1, tc), lambda k, j: (0, j))],
        out_specs=[fwd, fwd], out_shape=[out, out],
        compiler_params=_cp(2), name="hy_filter_spectrum")(h_re, h_im, h_re, h_im, tabs["f_blk"], wnorm)


def _hy_conv_kernel(tr_ref, ti_ref, kr_ref, ki_ref, f_ref, fi_ref, cr_ref, ci_ref, *, nk1):
    k1 = pl.program_id(0)
    n2 = tr_ref.shape[1]

    @pl.when(k1 < nk1)
    def _():
        s = jnp.dot(f_ref[...], jnp.concatenate([tr_ref[0], ti_ref[0]], axis=0), preferred_element_type=F32)
        sr, si = s[:n2], s[n2:]
        kr, ki = kr_ref[0], ki_ref[0]
        y = jnp.concatenate([sr * kr - si * ki, sr * ki + si * kr], axis=0).astype(BF16)
        c = jnp.dot(fi_ref[...], y, preferred_element_type=F32)
        cr_ref[0] = c[:n2].astype(cr_ref.dtype)
        ci_ref[0] = c[n2:].astype(ci_ref.dtype)

    @pl.when(k1 >= nk1)
    def _():
        cr_ref[...] = jnp.zeros(cr_ref.shape, cr_ref.dtype)
        ci_ref[...] = jnp.zeros(ci_ref.shape, ci_ref.dtype)


def _hy_conv(t_re, t_im, k_re, k_im, tabs, *, order, cols):
    nk1, k1p = tabs["nk1"], tabs["k1p"]
    tc = _tile(cols, 1024)
    nb = cols // tc
    sig = pl.BlockSpec((1, DFT_N2, tc), lambda k, j: (k, 0, j))
    flt = pl.BlockSpec((1, DFT_N2, tc), lambda k, j: (jnp.minimum(k, nk1 - 1), 0, order * nb + j))
    mat = pl.BlockSpec((2 * DFT_N2, 2 * DFT_N2), lambda k, j: (0, 0))
    out = jax.ShapeDtypeStruct((k1p, DFT_N2, cols), BF16)
    return pl.pallas_call(
        functools.partial(_hy_conv_kernel, nk1=nk1), grid=(k1p, nb),
        in_specs=[sig, sig, flt, flt, mat, mat], out_specs=[sig, sig], out_shape=[out, out],
        compiler_params=_cp(2), name="hy_conv")(t_re, t_im, k_re, k_im, tabs["f_blk"], tabs["fi_blk"])


def _idft_a_kernel(cr_ref, ci_ref, g_ref, x_ref, z_ref, d_ref, o_ref, ob_ref, *, n_sub, cols):
    for s in range(n_sub):
        sl = slice(s * cols, (s + 1) * cols)
        c = jnp.concatenate([cr_ref[:, sl], ci_ref[:, sl]], axis=0)
        y = jnp.dot(g_ref[s], c, preferred_element_type=F32)
        out = x_ref[:, sl] * (y + z_ref[:, sl] * d_ref[...])
        o_ref[:, sl] = out
        ob_ref[:, sl] = out.astype(ob_ref.dtype)


def _idft_a(c_re, c_im, tabs, x_mul, z_prev, d_term, *, seq, cols):
    h1, k1p = tabs["h1"], tabs["k1p"]
    n_sub = max(1, min(DFT_N2, 8192 // cols))
    wide = n_sub * cols
    spec_c = pl.BlockSpec((k1p, wide), lambda i: (0, i))
    spec_x = pl.BlockSpec((h1, wide), lambda i: (0, i))
    view = lambda a: a.reshape(h1, DFT_N2 * cols)
    o, ob = pl.pallas_call(
        functools.partial(_idft_a_kernel, n_sub=n_sub, cols=cols), grid=(DFT_N2 // n_sub,),
        in_specs=[spec_c, spec_c, pl.BlockSpec((n_sub, h1, 2 * k1p), lambda i: (i, 0, 0)), spec_x, spec_x,
                  pl.BlockSpec((1, cols), lambda i: (0, 0))],
        out_specs=[spec_x, spec_x],
        out_shape=[jax.ShapeDtypeStruct((h1, DFT_N2 * cols), F32),
                   jax.ShapeDtypeStruct((h1, DFT_N2 * cols), BF16)],
        compiler_params=_cp(1), name="idft_outer")(
            c_re.reshape(k1p, DFT_N2 * cols), c_im.reshape(k1p, DFT_N2 * cols), tabs["inv"],
            view(x_mul), view(z_prev), d_term.reshape(1, cols))
    return o.reshape(seq, cols), ob.reshape(seq, cols)


def _hyena(hy_u, conv_w, conv_b, filt_params, d_term, feats, deltas, tabs, *, seq, width):
    hwin, l1 = _hy_filters(feats, deltas, *filt_params, seq=seq, width=width, h1=tabs["h1"])
    n_cols = HY_ORDER * width
    l1 = l1[0, :n_cols] + l1[0, n_cols:]
    wnorm = (1.0 / (tabs["n"] * (l1 + EPS))).reshape(1, n_cols)
    h_re, h_im = _dft_a(hwin, tabs, cols=2 * n_cols)
    k_re, k_im = _hy_filter_spectrum(h_re, h_im, tabs, wnorm, cols=n_cols)
    x0, x1, v, v_bf = _hy_short_conv(hy_u, conv_w, conv_b, seq=seq, width=width)
    z, z_bf = v, v_bf
    for order, x_mul in enumerate((x0, x1)):
        t_re, t_im = _dft_a(z_bf, tabs, cols=width)
        c_re, c_im = _hy_conv(t_re, t_im, k_re, k_im, tabs, order=order, cols=width)
        z, z_bf = _idft_a(c_re, c_im, tabs, x_mul, z, d_term[order], seq=seq, cols=width)
    return z_bf


def _merge_kernel(gl_ref, oa_ref, ob_ref, oc_ref, od_ref, wg_ref, wb_ref, bg_ref, o_ref):
    gl = gl_ref[...]
    acc = None
    for i, o in enumerate((oa_ref, ob_ref, oc_ref, od_ref)):
        gate = jax.nn.sigmoid(jnp.dot(gl, wg_ref[i], preferred_element_type=F32) + bg_ref[i])
        term = gate * jnp.dot(o[...], wb_ref[i], preferred_element_type=F32)
        acc = term if acc is None else acc + term
    o_ref[...] = acc.astype(o_ref.dtype)


def _merge(proj, branches, w_gate, b_gate, w_branch, layer, *, gate_blk, tm=512, tn=1024):
    m, unit = branches[0].shape
    _, nbr, rank, n = w_gate.shape
    tm, tn = _tile(m, tm), _tile(n, tn)
    br = pl.BlockSpec((tm, unit), lambda j, i: (i, 0))
    return pl.pallas_call(
        _merge_kernel, grid=(n // tn, m // tm),
        in_specs=[pl.BlockSpec((tm, rank), lambda j, i: (i, gate_blk)), br, br, br, br,
                  pl.BlockSpec((None, nbr, rank, tn), lambda j, i: (layer, 0, 0, j)),
                  pl.BlockSpec((None, nbr, unit, tn), lambda j, i: (layer, 0, 0, j)),
                  pl.BlockSpec((nbr, 1, tn), lambda j, i: (0, 0, j))],
        out_specs=pl.BlockSpec((tm, tn), lambda j, i: (i, j)),
        out_shape=jax.ShapeDtypeStruct((m, n), BF16),
        compiler_params=_cp(2), name="merge")(proj, *branches, w_gate, w_branch, b_gate.reshape(nbr, 1, n))


def _xattn_kernel(h_ref, x_ref, wq_ref, kv_ref, wo_ref, gp_ref, gn_ref, xo_ref, ho_ref, *, heads, dh, qscale):
    q = (jnp.dot(h_ref[...], wq_ref[...], preferred_element_type=F32) * qscale).astype(BF16)
    kv = kv_ref[...]
    outs = []
    for hh in range(heads):
        k = kv[:, hh * dh:(hh + 1) * dh]
        v = kv[:, (heads + hh) * dh:(heads + hh + 1) * dh]
        s = lax.dot_general(q[:, hh * dh:(hh + 1) * dh], k, (((1,), (1,)), ((), ())),
                            preferred_element_type=F32)
        p = jnp.exp2(s - jnp.max(s, axis=-1, keepdims=True))
        o = jnp.dot(p.astype(BF16), v, preferred_element_type=F32) / jnp.sum(p, axis=-1, keepdims=True)
        outs.append(o.astype(BF16))
    y = jnp.dot(jnp.concatenate(outs, axis=1), wo_ref[...], preferred_element_type=F32)
    xn = x_ref[...] + _rms(y, gp_ref[...])
    xo_ref[...] = xn
    ho_ref[...] = _rms(xn, gn_ref[...]).astype(ho_ref.dtype)


def _cross_attention(h, x, kv, wq, wo, layer, g_post, g_next):
    m, d = x.shape
    dh, heads = HEAD_DIM, XA_HEADS
    n_mem = kv.shape[0]
    tm = _tile(m, 256)
    row = pl.BlockSpec((tm, d), lambda i: (i, 0))
    vec = pl.BlockSpec((1, d), lambda i: (0, 0))
    full = lambda shape: pl.BlockSpec(shape, lambda i: (0, 0))
    kern = functools.partial(_xattn_kernel, heads=heads, dh=dh, qscale=dh ** -0.5 * LOG2E)
    return pl.pallas_call(
        kern, grid=(m // tm,),
        in_specs=[row, row, pl.BlockSpec((None, d, heads * dh), lambda i: (layer, 0, 0)),
                  full((n_mem, 2 * heads * dh)), pl.BlockSpec((None, heads * dh, d), lambda i: (layer, 0, 0)), vec, vec],
        out_specs=[row, row],
        out_shape=[jax.ShapeDtypeStruct((m, d), F32), jax.ShapeDtypeStruct((m, d), BF16)],
        compiler_params=_cp(1), name="cross_attention")(
            h, x, wq, kv, wo, g_post.reshape(1, d), g_next.reshape(1, d))


def _rope_tables(seq):
    dh = HEAD_DIM
    n_rows = seq // GRID_W
    row = jnp.broadcast_to(jnp.arange(n_rows, dtype=F32)[:, None], (n_rows, GRID_W)).reshape(seq)
    col = jnp.broadcast_to(jnp.arange(GRID_W, dtype=F32)[None, :], (n_rows, GRID_W)).reshape(seq)
    axis_dim = dh // 2
    inv_freq = ROPE_THETA ** (-jnp.arange(0, axis_dim, 2, dtype=F32) / axis_dim)
    ang = jnp.stack([row[:, None] * inv_freq, col[:, None] * inv_freq], axis=1)
    ang = jnp.broadcast_to(ang[:, :, None, :], (seq, 2, 2, axis_dim // 2)).reshape(seq, dh)
    sign = jnp.where((jnp.arange(dh) % (dh // 2)) < dh // 4, -1.0, 1.0).astype(F32)
    return jnp.cos(ang), jnp.sin(ang) * sign[None, :]


def _hyena_tables(seq, width):
    bands = (HY_EMB - 1) // 2
    pos = jnp.arange(seq, dtype=F32)
    t = pos / (seq - 1)
    w = 2.0 * math.pi * pos / seq
    f = jnp.linspace(1e-4, bands - 1, bands, dtype=F32)
    ang = w[:, None] * f[None]
    feats = jnp.concatenate([t[:, None], jnp.cos(ang), -jnp.sin(ang)], axis=-1)
    deltas = jnp.abs(jnp.linspace(math.log(HY_DECAY_TARGET) / HY_SLOW_PCT,
                                  math.log(HY_DECAY_TARGET) / HY_FAST_PCT, width, dtype=F32))
    return feats, deltas


def kernel(x, mem, ffn1_pre_norm, ffn1_w1, ffn1_w3, ffn1_w2, ffn1_post_norm, mix_pre_norm, w_in, diff_lambda, diff_norm, gqa_q_norm, gqa_k_norm, ret_decay_logit, ret_norm, hy_conv_w, hy_conv_b, hy_w1, hy_b1, hy_w2, hy_b2, hy_w3, hy_sin_freq, hy_filter_bias, w_gate_up, b_gate, w_branch, w_out, mix_post_norm, xa_pre_norm, xa_mem_norm, xa_wq, xa_wkv, xa_wo, xa_post_norm, ffn2_pre_norm, ffn2_w1, ffn2_w3, ffn2_w2, ffn2_post_norm):
    batch, seq, d_model = x.shape
    assert batch == 1
    depth = w_in.shape[0]
    dh = HEAD_DIM
    unit = d_model // 4
    da_heads = unit // (2 * dh)
    gqa_heads = unit // dh
    gqa_kv = gqa_heads // 4
    ret_heads = unit // (2 * dh)
    hy_w = HY_ORDER + 1
    names = ("qa", "ka", "va", "qb", "kb", "vb", "qc", "kc", "vc", "gc", "hy", "gate")
    widths = (da_heads * 2 * dh, da_heads * 2 * dh, da_heads * 2 * dh, gqa_heads * dh, gqa_kv * dh,
              gqa_kv * dh, ret_heads * dh, ret_heads * dh, ret_heads * 2 * dh, ret_heads * 2 * dh,
              hy_w * unit, GATE_RANK)
    assert sum(widths) == w_in.shape[2]
    src = dict(zip(names, np.concatenate([[0], np.cumsum(widths)[:-1]]).tolist()))
    wid = dict(zip(names, widths))
    blk_w = dict(qa=2 * dh, ka=2 * dh, va=2 * dh, qb=wid["qb"], kb=wid["kb"], vb=wid["vb"], qc=dh, kc=dh,
                 vc=2 * dh, gc=2 * dh, gate=GATE_RANK)
    order_a = sorted(blk_w, key=lambda nm: -blk_w[nm])
    off, pos = {}, 0
    for nm in order_a:
        assert pos % blk_w[nm] == 0 and wid[nm] % blk_w[nm] == 0, nm
        off[nm] = pos
        pos += wid[nm]
    n_a = pos
    assert off["ka"] == off["qa"] + wid["qa"]

    att_scale = dh ** -0.5 * LOG2E
    colscale = jnp.ones((n_a,), F32)
    colscale = colscale.at[off["qa"]:off["qa"] + wid["qa"]].set(att_scale)
    colscale = colscale.at[off["kc"]:off["kc"] + wid["kc"]].set(dh ** -0.5)

    slopes = 2.0 ** (-8.0 * jnp.arange(1, da_heads + 1, dtype=F32) / da_heads)
    rope_cos, rope_sin = _rope_tables(seq)
    hy_feats, hy_deltas = _hyena_tables(seq, unit)
    tabs = _dft_tables(seq)

    ffn1_w = tuple(w.astype(BF16) for w in (ffn1_w1, ffn1_w3, ffn1_w2))
    ffn2_w = tuple(w.astype(BF16) for w in (ffn2_w1, ffn2_w3, ffn2_w2))
    w_gate_b, w_branch_b, w_out_b = w_gate_up.astype(BF16), w_branch.astype(BF16), w_out.astype(BF16)
    xa_wq_b, xa_wkv_b, xa_wo_b = xa_wq.astype(BF16), xa_wkv.astype(BF16), xa_wo.astype(BF16)

    xs = x.reshape(seq, d_model)
    mem2 = mem.reshape(mem.shape[1], d_model)
    h = _norm_cast(xs, ffn1_pre_norm[0])
    for l in range(depth):
        xs, h = _ffn(xs, h, *ffn1_w, l, ffn1_post_norm[l], mix_pre_norm[l])

        w_l = w_in[l]
        w_a = jnp.concatenate([w_l[:, src[nm]:src[nm] + wid[nm]] for nm in order_a], axis=1).astype(BF16)
        proj = _matmul(h, w_a, BF16, colscale)
        hy_u = _matmul(h, w_l[:, src["hy"]:src["gate"]].astype(BF16), F32)

        lam_init = 0.8 - 0.6 * math.exp(-0.3 * l)
        lp = diff_lambda[l]
        lam = jnp.exp(jnp.sum(lp[0] * lp[1])) - jnp.exp(jnp.sum(lp[2] * lp[3])) + lam_init
        scal = jnp.concatenate([lam.reshape(1), slopes * LOG2E]).astype(F32)
        top = _head_norm_max(proj, seq=seq, first_blk=off["qa"] // (2 * dh), n_blk=2 * da_heads)
        spread = 2.0 * jnp.max(jnp.sqrt(top[:da_heads] * top[da_heads:]), axis=1)
        reach = jnp.ceil((ALIBI_ZERO_LOG2 + spread) / (slopes * LOG2E))
        reach = jnp.where(reach < seq, jnp.maximum(reach, 0.0), seq).astype(jnp.int32)
        da_tq, da_tk = _tile(seq, DIFF_QUERY_BLOCK), _tile(seq // FLASH_SUBSTEPS, DIFF_KEY_BLOCK)
        schedule = _alibi_schedule(reach, seq=seq, tq=da_tq, span=FLASH_SUBSTEPS * da_tk)
        oa = _flash_attention(proj, proj, proj, seq=seq, heads=da_heads, n_maps=2, q_blk=off["qa"] // (2 * dh),
                              k_blk=off["ka"] // (2 * dh), v_blk=off["va"] // (2 * dh), tq=da_tq, tk=da_tk,
                              diff_args=(scal, diff_norm[l], 1.0 - lam_init, schedule))

        qg, kg, vg = _gqa_prep(proj, rope_cos, rope_sin, gqa_q_norm[l], gqa_k_norm[l], seq=seq,
                               q_w=wid["qb"], kv_w=wid["kb"], q_blk=off["qb"] // wid["qb"],
                               k_blk=off["kb"] // wid["kb"], v_blk=off["vb"] // wid["vb"], qscale=att_scale)
        ob = _flash_attention(qg, kg, vg, seq=seq, heads=gqa_kv, n_maps=gqa_heads // gqa_kv, q_blk=0, k_blk=0,
                              v_blk=0, tq=128, tk=GQA_KEY_BLOCK)

        log_g = -jax.nn.softplus(-ret_decay_logit[l].astype(F32))
        ret_args = dict(seq=seq, heads=ret_heads, q_blk=off["qc"] // dh, k_blk=off["kc"] // dh,
                        v_blk=off["vc"] // (2 * dh), gate_blk=off["gc"] // (2 * dh))
        y_bwd = _retention_pass(proj, _retention_tables(log_g[1], True), reverse=True, **ret_args)
        oc = _retention_pass(proj, _retention_tables(log_g[0], False), reverse=False, y_bwd=y_bwd,
                             gain=ret_norm[l], **ret_args)

        od = _hyena(hy_u, hy_conv_w[l], hy_conv_b[l],
                    (hy_w1[l], hy_b1[l], hy_w2[l], hy_b2[l], hy_w3[l], hy_sin_freq[l]),
                    hy_filter_bias[l], hy_feats, hy_deltas, tabs, seq=seq, width=unit)

        merged = _merge(proj, (oa, ob, oc, od), w_gate_b, b_gate[l], w_branch_b, l,
                        gate_blk=off["gate"] // GATE_RANK)
        y = _matmul(merged, w_out_b, F32, layer=l)
        xs, h = _resid_norm(xs, y, mix_post_norm[l], 1.0, xa_pre_norm[l])

        mem_n = _norm_cast(mem2, xa_mem_norm[l])
        kv = _matmul(mem_n, xa_wkv_b, BF16, layer=l)
        xs, h = _cross_attention(h, xs, kv, xa_wq_b, xa_wo_b, l, xa_post_norm[l], ffn2_pre_norm[l])

        g_next = ffn1_pre_norm[l + 1] if l + 1 < depth else ffn2_pre_norm[l]
        xs, h = _ffn(xs, h, *ffn2_w, l, ffn2_post_norm[l], g_next)
    return xs.reshape(batch, seq, d_model)
```

```python
import functools
import math

import numpy as np
import jax
import jax.numpy as jnp
from jax import lax
from jax.experimental import pallas as pl
from jax.experimental.pallas import tpu as pltpu

F32 = jnp.float32
BF16 = jnp.bfloat16

EPS = 1e-6
HEAD_DIM = 128
GATE_RANK = 512
GRID_W = 64
ROPE_THETA = 10000.0
RET_CHUNK = 128
XA_HEADS = 4
HY_ORDER = 2
HY_EMB = 33
HY_DECAY_TARGET = 1e-2
HY_FAST_PCT = 0.3
HY_SLOW_PCT = 1.5
LOG2E = math.log2(math.e)
DFT_N2 = 256
NEG_BIG = -1e30

VMEM_LIMIT_BYTES = 56 * 1024 * 1024


def _cp(n_axes):
    return pltpu.CompilerParams(dimension_semantics=("arbitrary",) * n_axes,
                                vmem_limit_bytes=VMEM_LIMIT_BYTES)


def _tile(n, pref, align=128):
    for t in range(min(n, pref), 0, -1):
        if n % t == 0 and t % align == 0:
            return t
    raise ValueError((n, pref, align))


def _rms(x, gain):
    return x * lax.rsqrt(jnp.mean(x * x, axis=-1, keepdims=True) + EPS) * gain


def _norm_kernel(x_ref, g_ref, h_ref):
    h_ref[...] = _rms(x_ref[...], g_ref[...]).astype(h_ref.dtype)


def _norm_cast(x, gain):
    m, d = x.shape
    tm = _tile(m, 256)
    row = pl.BlockSpec((tm, d), lambda i: (i, 0))
    vec = pl.BlockSpec((1, d), lambda i: (0, 0))
    return pl.pallas_call(
        _norm_kernel, grid=(m // tm,), in_specs=[row, vec], out_specs=row,
        out_shape=jax.ShapeDtypeStruct((m, d), BF16), compiler_params=_cp(1),
        name="norm_cast")(x, gain.reshape(1, d))


def _resid_norm_kernel(x_ref, y_ref, gp_ref, gn_ref, xo_ref, ho_ref, *, alpha):
    xn = x_ref[...] + alpha * _rms(y_ref[...].astype(F32), gp_ref[...])
    xo_ref[...] = xn
    ho_ref[...] = _rms(xn, gn_ref[...]).astype(ho_ref.dtype)


def _resid_norm(x, y, g_post, alpha, g_next):
    m, d = x.shape
    tm = _tile(m, 256)
    row = pl.BlockSpec((tm, d), lambda i: (i, 0))
    vec = pl.BlockSpec((1, d), lambda i: (0, 0))
    return pl.pallas_call(
        functools.partial(_resid_norm_kernel, alpha=alpha), grid=(m // tm,),
        in_specs=[row, row, vec, vec], out_specs=[row, row],
        out_shape=[jax.ShapeDtypeStruct((m, d), F32), jax.ShapeDtypeStruct((m, d), BF16)],
        compiler_params=_cp(1), name="resid_norm")(x, y, g_post.reshape(1, d), g_next.reshape(1, d))


def _mm_kernel(a_ref, w_ref, s_ref, o_ref):
    acc = jnp.dot(a_ref[...], w_ref[...], preferred_element_type=F32)
    o_ref[...] = (acc * s_ref[...]).astype(o_ref.dtype)


def _matmul(a, w, out_dtype, colscale=None, layer=None, tm=1024, tn=1024):
    m, k = a.shape
    n = w.shape[-1]
    tm, tn = _tile(m, tm), _tile(n, tn)
    if colscale is None:
        colscale = jnp.ones((n,), F32)
    if layer is None:
        w_spec = pl.BlockSpec((k, tn), lambda j, i: (0, j))
    else:
        w_spec = pl.BlockSpec((None, k, tn), lambda j, i: (layer, 0, j))
    return pl.pallas_call(
        _mm_kernel, grid=(n // tn, m // tm),
        in_specs=[pl.BlockSpec((tm, k), lambda j, i: (i, 0)), w_spec, pl.BlockSpec((1, tn), lambda j, i: (0, j))],
        out_specs=pl.BlockSpec((tm, tn), lambda j, i: (i, j)),
        out_shape=jax.ShapeDtypeStruct((m, n), out_dtype),
        compiler_params=_cp(2), name="matmul")(a, w, colscale.reshape(1, n).astype(F32))


def _ffn_up_kernel(h_ref, w1_ref, w3_ref, o_ref):
    h = h_ref[...]
    a = jnp.dot(h, w1_ref[...], preferred_element_type=F32)
    b = jnp.dot(h, w3_ref[...], preferred_element_type=F32)
    o_ref[...] = (a * jax.nn.sigmoid(a) * b).astype(o_ref.dtype)


def _ffn_up(h, w1, w3, layer, tm=1024, tn=512):
    m, k = h.shape
    n = w1.shape[-1]
    tm, tn = _tile(m, tm), _tile(n, tn)
    wspec = pl.BlockSpec((None, k, tn), lambda j, i: (layer, 0, j))
    return pl.pallas_call(
        _ffn_up_kernel, grid=(n // tn, m // tm),
        in_specs=[pl.BlockSpec((tm, k), lambda j, i: (i, 0)), wspec, wspec],
        out_specs=pl.BlockSpec((tm, tn), lambda j, i: (i, j)),
        out_shape=jax.ShapeDtypeStruct((m, n), BF16),
        compiler_params=_cp(2), name="ffn_up")(h, w1, w3)


def _ffn(x, h, w1, w3, w2, layer, g_post, g_next):
    u = _ffn_up(h, w1, w3, layer)
    y = _matmul(u, w2, BF16, layer=layer)
    return _resid_norm(x, y, g_post, 0.5, g_next)


FLASH_SUBSTEPS = 2
DIFF_QUERY_BLOCK, DIFF_KEY_BLOCK, GQA_KEY_BLOCK = 256, 1024, 2048
ALIBI_ZERO_LOG2 = 150.0


def _alibi_mixed_bases(tq, tk):
    step = math.gcd(tq, tk)
    return [b for b in range(-(tq // step) * step, tk + step, step) if 1 - tq < b < tk - 1], step


def _flash_kernel(*refs, n_maps, dh, tq, tk, nq, pairs, diff, post_scale):
    if diff:
        (n_ref, qb_ref, pr_ref, fst_ref, lst_ref, scal_ref, q_ref, k_ref, v_ref, g_ref, o_ref,
         s_buf, p_buf, al_buf, m_ref, acc_ref, l_ref, tbl_ref) = refs
        mixed, base_step = _alibi_mixed_bases(tq, tk)
    else:
        q_ref, k_ref, v_ref, o_ref, s_buf, p_buf, al_buf, m_ref, acc_ref = refs
    h, g = pl.program_id(0), pl.program_id(1)
    n_sub = FLASH_SUBSTEPS
    if diff:
        n_real = n_ref[h]

        def entry(e):
            e = jnp.maximum(e, 0)
            return qb_ref[h, e], pr_ref[h, e], fst_ref[h, e] == 1, lst_ref[h, e] == 1
    else:
        n_real = nq * pairs

        def entry(e):
            e = jnp.clip(e, 0, n_real - 1)
            return e // pairs, e % pairs, e % pairs == 0, e % pairs == pairs - 1

    @pl.when(g == 0)
    def _():
        s_buf[...] = jnp.zeros(s_buf.shape, F32)
        p_buf[...] = jnp.zeros(p_buf.shape, BF16)
        al_buf[...] = jnp.ones(al_buf.shape, F32)
        m_ref[...] = jnp.full(m_ref.shape, NEG_BIG, F32)
        acc_ref[...] = jnp.zeros(acc_ref.shape, F32)
        if diff:
            l_ref[...] = jnp.zeros(l_ref.shape, F32)
            d = lax.broadcasted_iota(jnp.int32, (tq, tk), 0) - lax.broadcasted_iota(jnp.int32, (tq, tk), 1)
            slope = scal_ref[1 + h]
            tbl_ref[0] = d.astype(F32) * slope
            tbl_ref[1] = (-d).astype(F32) * slope
            for i, mb in enumerate(mixed):
                tbl_ref[2 + i] = jnp.abs(d + mb).astype(F32) * slope

    def pipeline_step():
        for sub in range(n_sub):
            other = 1 - sub
            for c in range(n_maps):
                k_lo = c * dh if diff else 0
                s_buf[sub, c] = lax.dot_general(q_ref[:, c * dh:(c + 1) * dh],
                                                k_ref[sub * tk:(sub + 1) * tk, k_lo:k_lo + dh],
                                                (((1,), (1,)), ((), ())), preferred_element_type=F32)
            e = g - 1 + sub
            valid = jnp.logical_and(e >= 0, e < n_real)
            qb, pr, fst, _ = entry(e)
            kb = n_sub * pr + other
            par = qb % 2
            if diff:
                base = qb * tq - kb * tk
                below, above = base >= tk - 1, base <= 1 - tq
                sel = jnp.where(below, 0, jnp.where(above, 1, 2 + (base - mixed[0]) // base_step))
                bias = tbl_ref[jnp.clip(sel, 0, 1 + len(mixed))]
                basef = (jnp.zeros((tq, 1), jnp.int32) + base).astype(F32) * scal_ref[1 + h]
                cst = jnp.where(below, -basef, jnp.where(above, basef, 0.0))
            else:
                cst = jnp.zeros((tq, 1), F32)
            cst = jnp.where(valid, cst, -jnp.inf)
            for c in range(n_maps):
                s = s_buf[other, c]
                if diff:
                    s = s - bias
                m_prev = m_ref[par, c]
                if other == 0:
                    m_prev = jnp.where(fst, NEG_BIG, m_prev)
                m_new = jnp.maximum(m_prev, jnp.max(s, axis=-1, keepdims=True) + cst)
                alpha = jnp.exp2(m_prev - m_new)
                p = jnp.exp2(s - (m_new - cst))
                if diff:
                    l_ref[par, c] = alpha * l_ref[par, c] + jnp.sum(p, axis=-1, keepdims=True)
                m_ref[par, c] = m_new
                al_buf[other, c] = alpha
                p_buf[other, c] = p.astype(BF16)
            v = v_ref[sub * tk:(sub + 1) * tk, :]
            for c in range(n_maps):
                acc_ref[c] = al_buf[sub, c] * acc_ref[c] + jnp.dot(p_buf[sub, c], v, preferred_element_type=F32)

    if diff:
        pl.when(g <= n_real)(pipeline_step)
    else:
        pipeline_step()

    qb_w, _, _, last_w = entry(g - 1)

    @pl.when(jnp.logical_and(jnp.logical_and(g >= 1, g <= n_real), last_w))
    def _():
        if diff:
            par_w = qb_w % 2
            o = acc_ref[0] / l_ref[par_w, 0] - scal_ref[0] * (acc_ref[1] / l_ref[par_w, 1])
            o_ref[...] = (_rms(o, g_ref[...]) * post_scale).astype(o_ref.dtype)
        else:
            for r in range(n_maps):
                a = acc_ref[r]
                o_ref[:, r * dh:(r + 1) * dh] = (a[:, :dh] / a[:, dh:]).astype(o_ref.dtype)


def _alibi_schedule(reach, *, seq, tq, span):
    nq, pairs = seq // tq, seq // span
    q_lo = jnp.arange(nq, dtype=jnp.int32) * tq
    lo = jnp.clip((q_lo[None, :] - reach[:, None]) // span, 0, pairs - 1)
    hi = jnp.clip((q_lo[None, :] + (tq - 1) + reach[:, None]) // span, 0, pairs - 1)
    cnt = hi - lo + 1
    ends = jnp.cumsum(cnt, axis=1)
    starts = ends - cnt
    n_real = ends[:, -1]
    e = jnp.minimum(jnp.arange(nq * pairs + 1, dtype=jnp.int32)[None, :], n_real[:, None] - 1)
    qb = jnp.sum((ends[:, None, :] <= e[:, :, None]).astype(jnp.int32), axis=2)
    pick = qb[:, :, None] == jnp.arange(nq, dtype=jnp.int32)[None, None, :]
    st = jnp.sum(jnp.where(pick, starts[:, None, :], 0), axis=2)
    pr = jnp.sum(jnp.where(pick, lo[:, None, :], 0), axis=2) + e - st
    fst = (e == st).astype(jnp.int32)
    lst = (e == jnp.sum(jnp.where(pick, ends[:, None, :], 0), axis=2) - 1).astype(jnp.int32)
    return n_real.astype(jnp.int32), qb, pr.astype(jnp.int32), fst, lst


def _flash_attention(q, k, v, *, seq, heads, n_maps, q_blk, k_blk, v_blk, tq, tk, diff_args=None):
    dh = HEAD_DIM
    n_sub = FLASH_SUBSTEPS
    diff = diff_args is not None
    tq = _tile(seq, tq)
    tk = _tile(seq // n_sub, tk)
    nq, pairs = seq // tq, seq // (n_sub * tk)
    kern = functools.partial(_flash_kernel, n_maps=n_maps, dh=dh, tq=tq, tk=tk, nq=nq, pairs=pairs, diff=diff,
                             post_scale=diff_args[2] if diff else None)
    scratch = [pltpu.VMEM((n_sub, n_maps, tq, tk), F32), pltpu.VMEM((n_sub, n_maps, tq, tk), BF16),
               pltpu.VMEM((n_sub, n_maps, tq, 1), F32), pltpu.VMEM((2, n_maps, tq, 1), F32),
               pltpu.VMEM((n_maps, tq, 2 * dh), F32)]
    out_shape = jax.ShapeDtypeStruct((seq, heads * n_maps * dh), BF16)
    if not diff:
        last = nq * pairs - 1
        return pl.pallas_call(
            kern, grid=(heads, nq * pairs + 1),
            in_specs=[pl.BlockSpec((tq, n_maps * dh), lambda h, g: (jnp.minimum(g, last) // pairs, q_blk + h)),
                      pl.BlockSpec((n_sub * tk, dh), lambda h, g: (jnp.minimum(g, last) % pairs, k_blk + h)),
                      pl.BlockSpec((n_sub * tk, 2 * dh), lambda h, g: (jnp.maximum(g - 1, 0) % pairs, v_blk + h))],
            out_specs=pl.BlockSpec((tq, n_maps * dh), lambda h, g: (jnp.maximum(g - 1, 0) // pairs, h)),
            out_shape=out_shape, scratch_shapes=scratch, compiler_params=_cp(2), name="gqa_attention")(q, k, v)
    scal, gain, _, schedule = diff_args
    scratch += [pltpu.VMEM((2, n_maps, tq, 1), F32),
                pltpu.VMEM((2 + len(_alibi_mixed_bases(tq, tk)[0]), tq, tk), F32)]
    prev = lambda g: jnp.maximum(g - 1, 0)
    grid_spec = pltpu.PrefetchScalarGridSpec(
        num_scalar_prefetch=5, grid=(heads, nq * pairs + 1),
        in_specs=[pl.BlockSpec(memory_space=pltpu.SMEM),
                  pl.BlockSpec((tq, n_maps * dh), lambda h, g, n, qb, pr, fs, ls: (qb[h, g], q_blk + h)),
                  pl.BlockSpec((n_sub * tk, n_maps * dh), lambda h, g, n, qb, pr, fs, ls: (pr[h, g], k_blk + h)),
                  pl.BlockSpec((n_sub * tk, 2 * dh), lambda h, g, n, qb, pr, fs, ls: (pr[h, prev(g)], v_blk + h)),
                  pl.BlockSpec((1, 2 * dh), lambda h, g, n, qb, pr, fs, ls: (0, 0))],
        out_specs=pl.BlockSpec((tq, n_maps * dh), lambda h, g, n, qb, pr, fs, ls: (qb[h, prev(g)], h)),
        scratch_shapes=scratch)
    return pl.pallas_call(kern, grid_spec=grid_spec, out_shape=out_shape, compiler_params=_cp(2),
                          name="diff_attention")(*schedule, scal, q, k, v, gain.reshape(1, 2 * dh))


def _norm_max_kernel(x_ref, o_ref, *, dh):
    @pl.when(pl.program_id(1) == 0)
    def _():
        o_ref[...] = jnp.zeros(o_ref.shape, F32)

    x = x_ref[...].astype(F32)
    sq = x * x
    for c in range(x.shape[1] // dh):
        top = jnp.max(jnp.sum(sq[:, c * dh:(c + 1) * dh], axis=1, keepdims=True), axis=0, keepdims=True)
        o_ref[0, :, c * dh:(c + 1) * dh] = jnp.maximum(o_ref[0, :, c * dh:(c + 1) * dh], top)


def _head_norm_max(proj, *, seq, first_blk, n_blk):
    dh = HEAD_DIM
    tm = _tile(seq, 2048)
    out = pl.pallas_call(
        functools.partial(_norm_max_kernel, dh=dh), grid=(n_blk, seq // tm),
        in_specs=[pl.BlockSpec((tm, 2 * dh), lambda b, i: (i, first_blk + b))],
        out_specs=pl.BlockSpec((1, 8, 2 * dh), lambda b, i: (b, 0, 0)),
        out_shape=jax.ShapeDtypeStruct((n_blk, 8, 2 * dh), F32),
        compiler_params=_cp(2), name="head_norm_max")(proj)
    return out[:, 0, ::dh]


def _gqa_prep_kernel(q_ref, k_ref, v_ref, cos_ref, sin_ref, gq_ref, gk_ref, qo_ref, ko_ref, vo_ref, *, qscale):
    cos = cos_ref[...]
    sin = sin_ref[...]
    dh = cos.shape[-1]
    lane = lax.broadcasted_iota(jnp.int32, cos.shape, 1)
    first_half = (lane % (dh // 2)) < (dh // 4)

    def norm_rope(xh, gain):
        xn = _rms(xh.astype(F32), gain)
        rot = jnp.where(first_half, pltpu.roll(xn, dh - dh // 4, 1), pltpu.roll(xn, dh // 4, 1))
        return xn * cos + rot * sin

    for hh in range(q_ref.shape[1] // dh):
        sl = slice(hh * dh, (hh + 1) * dh)
        qo_ref[:, sl] = (norm_rope(q_ref[:, sl], gq_ref[...]) * qscale).astype(qo_ref.dtype)
    for hh in range(k_ref.shape[1] // dh):
        sl = slice(hh * dh, (hh + 1) * dh)
        ko_ref[:, sl] = norm_rope(k_ref[:, sl], gk_ref[...]).astype(ko_ref.dtype)
        vo_ref[:, 2 * hh * dh:(2 * hh + 1) * dh] = v_ref[:, sl]
        vo_ref[:, (2 * hh + 1) * dh:(2 * hh + 2) * dh] = jnp.ones((v_ref.shape[0], dh), vo_ref.dtype)


def _gqa_prep(proj, cos, sin_signed, gq, gk, *, seq, q_w, kv_w, q_blk, k_blk, v_blk, qscale):
    dh = HEAD_DIM
    tm = _tile(seq, 512)
    vec = pl.BlockSpec((1, dh), lambda i: (0, 0))
    tab = pl.BlockSpec((tm, dh), lambda i: (i, 0))
    return pl.pallas_call(
        functools.partial(_gqa_prep_kernel, qscale=qscale), grid=(seq // tm,),
        in_specs=[pl.BlockSpec((tm, q_w), lambda i: (i, q_blk)),
                  pl.BlockSpec((tm, kv_w), lambda i: (i, k_blk)),
                  pl.BlockSpec((tm, kv_w), lambda i: (i, v_blk)), tab, tab, vec, vec],
        out_specs=[pl.BlockSpec((tm, q_w), lambda i: (i, 0)), pl.BlockSpec((tm, kv_w), lambda i: (i, 0)),
                   pl.BlockSpec((tm, 2 * kv_w), lambda i: (i, 0))],
        out_shape=[jax.ShapeDtypeStruct((seq, q_w), BF16), jax.ShapeDtypeStruct((seq, kv_w), BF16),
                   jax.ShapeDtypeStruct((seq, 2 * kv_w), BF16)],
        compiler_params=_cp(1), name="gqa_prep")(proj, proj, proj, cos, sin_signed, gq.reshape(1, dh),
                                                gk.reshape(1, dh))


def _ret_kernel(gch_ref, q_ref, k_ref, v_ref, dm_ref, xi_ref, ze_ref, *rest, chunk, n_sub, reverse, final):
    if final:
        yb_ref, gate_ref, gain_ref, o_ref, st_ref = rest
    else:
        o_ref, st_ref = rest
    h, n = pl.program_id(0), pl.program_id(1)

    @pl.when(n == 0)
    def _():
        st_ref[...] = jnp.zeros(st_ref.shape, F32)

    dm, xi, ze = dm_ref[0], xi_ref[0], ze_ref[0]
    g_chunk = gch_ref[h]
    for c in (range(n_sub - 1, -1, -1) if reverse else range(n_sub)):
        sl = slice(c * chunk, (c + 1) * chunk)
        q, k, v = q_ref[sl, :], k_ref[sl, :], v_ref[sl, :]
        s = lax.dot_general(q, k, (((1,), (1,)), ((), ())), preferred_element_type=F32) * dm
        st = st_ref[...]
        y = jnp.dot(s.astype(BF16), v, preferred_element_type=F32)
        y = y + jnp.dot((q.astype(F32) * xi).astype(BF16), st.astype(BF16), preferred_element_type=F32)
        kz = (k.astype(F32) * ze).astype(BF16)
        st_ref[...] = g_chunk * st + lax.dot_general(kz, v, (((0,), (0,)), ((), ())),
                                                     preferred_element_type=F32)
        if final:
            gate = gate_ref[sl, :].astype(F32)
            y = _rms(y + yb_ref[sl, :], gain_ref[...]) * (gate * jax.nn.sigmoid(gate))
        o_ref[sl, :] = y.astype(o_ref.dtype)


def _retention_pass(proj, tables, *, seq, heads, q_blk, k_blk, v_blk, gate_blk, reverse, y_bwd=None,
                    gain=None):
    dk, dv, chunk = HEAD_DIM, 2 * HEAD_DIM, RET_CHUNK
    g_chunk, dmask, xi, zeta = tables
    rows = _tile(seq, 8 * chunk)
    n_steps = seq // rows
    final = y_bwd is not None
    blk = (lambda n: n_steps - 1 - n) if reverse else (lambda n: n)
    head_tab = lambda shape: pl.BlockSpec((1,) + shape, lambda h, n: (h, 0, 0))
    in_specs = [pl.BlockSpec(memory_space=pltpu.SMEM),
                pl.BlockSpec((rows, dk), lambda h, n: (blk(n), q_blk + h)),
                pl.BlockSpec((rows, dk), lambda h, n: (blk(n), k_blk + h)),
                pl.BlockSpec((rows, dv), lambda h, n: (blk(n), v_blk + h)),
                head_tab((chunk, chunk)), head_tab((chunk, 1)), head_tab((chunk, 1))]
    args = [g_chunk, proj, proj, proj, dmask, xi, zeta]
    if final:
        in_specs += [pl.BlockSpec((rows, dv), lambda h, n: (blk(n), h)),
                     pl.BlockSpec((rows, dv), lambda h, n: (blk(n), gate_blk + h)),
                     pl.BlockSpec((1, dv), lambda h, n: (0, 0))]
        args += [y_bwd, proj, gain.reshape(1, dv)]
    kern = functools.partial(_ret_kernel, chunk=chunk, n_sub=rows // chunk, reverse=reverse, final=final)
    return pl.pallas_call(
        kern, grid=(heads, n_steps), in_specs=in_specs,
        out_specs=pl.BlockSpec((rows, dv), lambda h, n: (blk(n), h)),
        out_shape=jax.ShapeDtypeStruct((seq, heads * dv), BF16 if final else F32),
        scratch_shapes=[pltpu.VMEM((dk, dv), F32)],
        compiler_params=_cp(2), name="retention")(*args)


def _retention_tables(log_g, reverse):
    c = RET_CHUNK
    idx = jnp.arange(c, dtype=F32)
    diff = idx[:, None] - idx[None, :]
    if reverse:
        diff = -diff
        mask = diff > 0
        xi = jnp.exp((c - idx)[None, :] * log_g[:, None])
        zeta = jnp.exp(idx[None, :] * log_g[:, None])
    else:
        mask = diff >= 0
        xi = jnp.exp((idx + 1.0)[None, :] * log_g[:, None])
        zeta = jnp.exp((c - 1 - idx)[None, :] * log_g[:, None])
    dmask = jnp.where(mask[None], jnp.exp(jnp.where(mask, diff, 0.0)[None] * log_g[:, None, None]), 0.0)
    return jnp.exp(c * log_g), dmask, xi[:, :, None], zeta[:, :, None]


def _dft_tables(seq):
    n = 2 * seq
    n2 = DFT_N2
    n1 = n // n2
    h1 = n1 // 2
    nk1 = n1 // 2 + 1
    k1p = -(-nk1 // 16) * 16
    a2 = np.arange(n2)[:, None, None]
    k1 = np.arange(nk1)[None, :, None]
    a1 = np.arange(h1)[None, None, :]
    theta = 2.0 * np.pi * ((a1 * k1 % n1) / n1 + (a2 * k1 % n) / n)
    fwd = np.zeros((n2, 2 * k1p, h1))
    fwd[:, :nk1] = np.cos(theta)
    fwd[:, k1p:k1p + nk1] = -np.sin(theta)
    weight = np.where((np.arange(nk1) == 0) | (np.arange(nk1) == n1 // 2), 1.0, 2.0)[None, :, None]
    inv = np.zeros((n2, h1, 2 * k1p))
    inv[:, :, :nk1] = np.transpose(weight * np.cos(theta), (0, 2, 1))
    inv[:, :, k1p:k1p + nk1] = np.transpose(-weight * np.sin(theta), (0, 2, 1))
    ang = 2.0 * np.pi * (np.arange(n2)[:, None] * np.arange(n2)[None, :] % n2) / n2
    c, s = np.cos(ang), np.sin(ang)
    f_blk = np.block([[c, s], [-s, c]])
    fi_blk = np.block([[c, -s], [s, c]])
    as_bf16 = lambda a: jnp.asarray(a.astype(BF16))
    return dict(n=n, n1=n1, h1=h1, nk1=nk1, k1p=k1p, fwd=as_bf16(fwd), inv=as_bf16(inv),
                f_blk=as_bf16(f_blk), fi_blk=as_bf16(fi_blk))


def _hy_filter_kernel(feat_ref, w1_ref, b1_ref, w2_ref, b2_ref, w3_ref, fr_ref, dl_ref, h_ref, l1_ref,
                      *, width, h1):
    i = pl.program_id(0)
    feats = feat_ref[...]
    fr = fr_ref[...]
    hid = jnp.sin(fr * (jnp.dot(feats.astype(BF16), w1_ref[...], preferred_element_type=F32) + b1_ref[...]))
    hid = jnp.sin(fr * (jnp.dot(hid.astype(BF16), w2_ref[...], preferred_element_type=F32) + b2_ref[...]))
    h = jnp.dot(hid.astype(BF16), w3_ref[...], preferred_element_type=F32)
    window = jnp.exp(-feats[:, 0:1] * dl_ref[...])
    n_groups = h.shape[1] // width
    row = lax.broadcasted_iota(jnp.int32, (h.shape[0], width), 0)
    not_lag0 = jnp.logical_or(row > 0, i > 0)

    @pl.when(i == 0)
    def _():
        l1_ref[...] = jnp.zeros(l1_ref.shape, F32)

    for g in range(n_groups):
        sl = slice(g * width, (g + 1) * width)
        hg = h[:, sl] * window
        if g >= n_groups // 2:
            hg = jnp.where(not_lag0, hg, 0.0)
        l1_ref[:, sl] += jnp.sum(jnp.abs(hg), axis=0, keepdims=True)
        hg = hg.astype(h_ref.dtype)
        for s in range(h.shape[0] // h1):
            lo = s * h.shape[1] + g * width
            h_ref[:, lo:lo + width] = hg[s * h1:(s + 1) * h1, :]


def _hy_filters(feats, deltas, w1, b1, w2, b2, w3, freq, *, seq, width, h1):
    hidden = w1.shape[1]
    emb = -(-w1.shape[0] // 128) * 128
    feats = jnp.pad(feats, ((0, 0), (0, emb - feats.shape[1])))
    feats = feats.reshape(h1, DFT_N2, emb).transpose(1, 0, 2).reshape(seq, emb)
    w1 = jnp.pad(w1, ((0, emb - w1.shape[0]), (0, 0)))
    cols = w3.shape[1]
    tl = max(_tile(seq, 512), h1)
    n_sub = tl // h1
    full = lambda shape: pl.BlockSpec(shape, lambda i: (0,) * len(shape))
    return pl.pallas_call(
        functools.partial(_hy_filter_kernel, width=width, h1=h1), grid=(seq // tl,),
        in_specs=[pl.BlockSpec((tl, emb), lambda i: (i, 0)), full((emb, hidden)), full((1, hidden)),
                  full((hidden, hidden)), full((1, hidden)), full((hidden, cols)), full((1, hidden)),
                  full((1, width))],
        out_specs=[pl.BlockSpec((h1, n_sub * cols), lambda i: (0, i)), full((1, cols))],
        out_shape=[jax.ShapeDtypeStruct((h1, DFT_N2 * cols), BF16), jax.ShapeDtypeStruct((1, cols), F32)],
        compiler_params=_cp(1), name="hy_filters")(
            feats, w1.astype(BF16), b1.reshape(1, hidden), w2.astype(BF16), b2.reshape(1, hidden),
            w3.astype(BF16), freq.reshape(1, hidden), deltas.reshape(1, width))


def _hy_short_conv_kernel(u_ref, up_ref, un_ref, w_ref, b_ref, x0_ref, x1_ref, v_ref, vb_ref):
    i = pl.program_id(0)
    u = u_ref[...]
    tm = u.shape[0]
    row = lax.broadcasted_iota(jnp.int32, u.shape, 0)
    prev_row = jnp.where(i > 0, up_ref[7:8, :], 0.0)
    next_row = jnp.where(i < pl.num_programs(0) - 1, un_ref[0:1, :], 0.0)
    before = jnp.where(row == 0, prev_row, pltpu.roll(u, 1, 0))
    after = jnp.where(row == tm - 1, next_row, pltpu.roll(u, tm - 1, 0))
    y = before * w_ref[0:1, :] + u * w_ref[1:2, :] + after * w_ref[2:3, :] + b_ref[...]
    wd = x0_ref.shape[1]
    x0_ref[...] = y[:, :wd]
    x1_ref[...] = y[:, wd:2 * wd]
    v_ref[...] = y[:, 2 * wd:]
    vb_ref[...] = y[:, 2 * wd:].astype(vb_ref.dtype)


def _hy_short_conv(u, w, b, *, seq, width):
    cols = u.shape[1]
    tm = _tile(seq, 256)
    nb8 = tm // 8
    last8 = seq // 8 - 1
    out = pl.BlockSpec((tm, width), lambda i: (i, 0))
    return pl.pallas_call(
        _hy_short_conv_kernel, grid=(seq // tm,),
        in_specs=[pl.BlockSpec((tm, cols), lambda i: (i, 0)),
                  pl.BlockSpec((8, cols), lambda i: (jnp.maximum(i * nb8 - 1, 0), 0)),
                  pl.BlockSpec((8, cols), lambda i: (jnp.minimum((i + 1) * nb8, last8), 0)),
                  pl.BlockSpec((3, cols), lambda i: (0, 0)), pl.BlockSpec((1, cols), lambda i: (0, 0))],
        out_specs=[out, out, out, out],
        out_shape=[jax.ShapeDtypeStruct((seq, width), F32)] * 3 + [jax.ShapeDtypeStruct((seq, width), BF16)],
        compiler_params=_cp(1), name="hy_short_conv")(u, u, u, w, b.reshape(1, cols))


def _dft_a_kernel(x_ref, g_ref, re_ref, im_ref, *, n_sub, cols, k1p):
    for s in range(n_sub):
        sl = slice(s * cols, (s + 1) * cols)
        t = jnp.dot(g_ref[s], x_ref[:, sl], preferred_element_type=F32)
        re_ref[:, sl] = t[:k1p].astype(re_ref.dtype)
        im_ref[:, sl] = t[k1p:].astype(im_ref.dtype)


def _dft_a(x, tabs, *, cols):
    h1, k1p = tabs["h1"], tabs["k1p"]
    xv = x.reshape(h1, DFT_N2 * cols)
    n_sub = max(1, min(DFT_N2, 8192 // cols))
    spec_o = pl.BlockSpec((k1p, n_sub * cols), lambda i: (0, i))
    out = jax.ShapeDtypeStruct((k1p, DFT_N2 * cols), BF16)
    re, im = pl.pallas_call(
        functools.partial(_dft_a_kernel, n_sub=n_sub, cols=cols, k1p=k1p), grid=(DFT_N2 // n_sub,),
        in_specs=[pl.BlockSpec((h1, n_sub * cols), lambda i: (0, i)),
                  pl.BlockSpec((n_sub, 2 * k1p, h1), lambda i: (i, 0, 0))],
        out_specs=[spec_o, spec_o], out_shape=[out, out],
        compiler_params=_cp(1), name="dft_outer")(xv, tabs["fwd"])
    return re.reshape(k1p, DFT_N2, cols), im.reshape(k1p, DFT_N2, cols)


def _hy_filter_spec_kernel(fr_ref, fi_ref, br_ref, bi_ref, f_ref, w_ref, kr_ref, ki_ref):
    n2 = fr_ref.shape[1]
    f = f_ref[...]
    sf = jnp.dot(f, jnp.concatenate([fr_ref[0], fi_ref[0]], axis=0), preferred_element_type=F32)
    sb = jnp.dot(f, jnp.concatenate([br_ref[0], bi_ref[0]], axis=0), preferred_element_type=F32)
    w = w_ref[...]
    kr_ref[0] = (sf[:n2] + sb[:n2]) * w
    ki_ref[0] = (sf[n2:] - sb[n2:]) * w


def _hy_filter_spectrum(h_re, h_im, tabs, wnorm, *, cols):
    nk1 = tabs["nk1"]
    tc = _tile(cols, 1024)
    nb = cols // tc
    fwd = pl.BlockSpec((1, DFT_N2, tc), lambda k, j: (k, 0, j))
    bwd = pl.BlockSpec((1, DFT_N2, tc), lambda k, j: (k, 0, nb + j))
    out = jax.ShapeDtypeStruct((nk1, DFT_N2, cols), F32)
    return pl.pallas_call(
        _hy_filter_spec_kernel, grid=(nk1, nb),
        in_specs=[fwd, fwd, bwd, bwd, pl.BlockSpec((2 * DFT_N2, 2 * DFT_N2), lambda k, j: (0, 0)),
                  pl.BlockSpec((1, tc), lambda k, j: (0, j))],
        out_specs=[fwd, fwd], out_shape=[out, out],
        compiler_params=_cp(2), name="hy_filter_spectrum")(h_re, h_im, h_re, h_im, tabs["f_blk"], wnorm)


def _hy_conv_kernel(tr_ref, ti_ref, kr_ref, ki_ref, f_ref, fi_ref, cr_ref, ci_ref, *, nk1):
    k1 = pl.program_id(0)
    n2 = tr_ref.shape[1]

    @pl.when(k1 < nk1)
    def _():
        s = jnp.dot(f_ref[...], jnp.concatenate([tr_ref[0], ti_ref[0]], axis=0), preferred_element_type=F32)
        sr, si = s[:n2], s[n2:]
        kr, ki = kr_ref[0], ki_ref[0]
        y = jnp.concatenate([sr * kr - si * ki, sr * ki + si * kr], axis=0).astype(BF16)
        c = jnp.dot(fi_ref[...], y, preferred_element_type=F32)
        cr_ref[0] = c[:n2].astype(cr_ref.dtype)
        ci_ref[0] = c[n2:].astype(ci_ref.dtype)

    @pl.when(k1 >= nk1)
    def _():
        cr_ref[...] = jnp.zeros(cr_ref.shape, cr_ref.dtype)
        ci_ref[...] = jnp.zeros(ci_ref.shape, ci_ref.dtype)


def _hy_conv(t_re, t_im, k_re, k_im, tabs, *, order, cols):
    nk1, k1p = tabs["nk1"], tabs["k1p"]
    tc = _tile(cols, 1024)
    nb = cols // tc
    sig = pl.BlockSpec((1, DFT_N2, tc), lambda k, j: (k, 0, j))
    flt = pl.BlockSpec((1, DFT_N2, tc), lambda k, j: (jnp.minimum(k, nk1 - 1), 0, order * nb + j))
    mat = pl.BlockSpec((2 * DFT_N2, 2 * DFT_N2), lambda k, j: (0, 0))
    out = jax.ShapeDtypeStruct((k1p, DFT_N2, cols), BF16)
    return pl.pallas_call(
        functools.partial(_hy_conv_kernel, nk1=nk1), grid=(k1p, nb),
        in_specs=[sig, sig, flt, flt, mat, mat], out_specs=[sig, sig], out_shape=[out, out],
        compiler_params=_cp(2), name="hy_conv")(t_re, t_im, k_re, k_im, tabs["f_blk"], tabs["fi_blk"])


def _idft_a_kernel(cr_ref, ci_ref, g_ref, x_ref, z_ref, d_ref, o_ref, ob_ref, *, n_sub, cols):
    for s in range(n_sub):
        sl = slice(s * cols, (s + 1) * cols)
        c = jnp.concatenate([cr_ref[:, sl], ci_ref[:, sl]], axis=0)
        y = jnp.dot(g_ref[s], c, preferred_element_type=F32)
        out = x_ref[:, sl] * (y + z_ref[:, sl] * d_ref[...])
        o_ref[:, sl] = out
        ob_ref[:, sl] = out.astype(ob_ref.dtype)


def _idft_a(c_re, c_im, tabs, x_mul, z_prev, d_term, *, seq, cols):
    h1, k1p = tabs["h1"], tabs["k1p"]
    n_sub = max(1, min(DFT_N2, 8192 // cols))
    wide = n_sub * cols
    spec_c = pl.BlockSpec((k1p, wide), lambda i: (0, i))
    spec_x = pl.BlockSpec((h1, wide), lambda i: (0, i))
    view = lambda a: a.reshape(h1, DFT_N2 * cols)
    o, ob = pl.pallas_call(
        functools.partial(_idft_a_kernel, n_sub=n_sub, cols=cols), grid=(DFT_N2 // n_sub,),
        in_specs=[spec_c, spec_c, pl.BlockSpec((n_sub, h1, 2 * k1p), lambda i: (i, 0, 0)), spec_x, spec_x,
                  pl.BlockSpec((1, cols), lambda i: (0, 0))],
        out_specs=[spec_x, spec_x],
        out_shape=[jax.ShapeDtypeStruct((h1, DFT_N2 * cols), F32),
                   jax.ShapeDtypeStruct((h1, DFT_N2 * cols), BF16)],
        compiler_params=_cp(1), name="idft_outer")(
            c_re.reshape(k1p, DFT_N2 * cols), c_im.reshape(k1p, DFT_N2 * cols), tabs["inv"],
            view(x_mul), view(z_prev), d_term.reshape(1, cols))
    return o.reshape(seq, cols), ob.reshape(seq, cols)


def _hyena(hy_u, conv_w, conv_b, filt_params, d_term, feats, deltas, tabs, *, seq, width):
    hwin, l1 = _hy_filters(feats, deltas, *filt_params, seq=seq, width=width, h1=tabs["h1"])
    n_cols = HY_ORDER * width
    l1 = l1[0, :n_cols] + l1[0, n_cols:]
    wnorm = (1.0 / (tabs["n"] * (l1 + EPS))).reshape(1, n_cols)
    h_re, h_im = _dft_a(hwin, tabs, cols=2 * n_cols)
    k_re, k_im = _hy_filter_spectrum(h_re, h_im, tabs, wnorm, cols=n_cols)
    x0, x1, v, v_bf = _hy_short_conv(hy_u, conv_w, conv_b, seq=seq, width=width)
    z, z_bf = v, v_bf
    for order, x_mul in enumerate((x0, x1)):
        t_re, t_im = _dft_a(z_bf, tabs, cols=width)
        c_re, c_im = _hy_conv(t_re, t_im, k_re, k_im, tabs, order=order, cols=width)
        z, z_bf = _idft_a(c_re, c_im, tabs, x_mul, z, d_term[order], seq=seq, cols=width)
    return z_bf


def _merge_kernel(gl_ref, oa_ref, ob_ref, oc_ref, od_ref, wg_ref, wb_ref, bg_ref, o_ref):
    gl = gl_ref[...]
    acc = None
    for i, o in enumerate((oa_ref, ob_ref, oc_ref, od_ref)):
        gate = jax.nn.sigmoid(jnp.dot(gl, wg_ref[i], preferred_element_type=F32) + bg_ref[i])
        term = gate * jnp.dot(o[...], wb_ref[i], preferred_element_type=F32)
        acc = term if acc is None else acc + term
    o_ref[...] = acc.astype(o_ref.dtype)


def _merge(proj, branches, w_gate, b_gate, w_branch, layer, *, gate_blk, tm=512, tn=1024):
    m, unit = branches[0].shape
    _, nbr, rank, n = w_gate.shape
    tm, tn = _tile(m, tm), _tile(n, tn)
    br = pl.BlockSpec((tm, unit), lambda j, i: (i, 0))
    return pl.pallas_call(
        _merge_kernel, grid=(n // tn, m // tm),
        in_specs=[pl.BlockSpec((tm, rank), lambda j, i: (i, gate_blk)), br, br, br, br,
                  pl.BlockSpec((None, nbr, rank, tn), lambda j, i: (layer, 0, 0, j)),
                  pl.BlockSpec((None, nbr, unit, tn), lambda j, i: (layer, 0, 0, j)),
                  pl.BlockSpec((nbr, 1, tn), lambda j, i: (0, 0, j))],
        out_specs=pl.BlockSpec((tm, tn), lambda j, i: (i, j)),
        out_shape=jax.ShapeDtypeStruct((m, n), BF16),
        compiler_params=_cp(2), name="merge")(proj, *branches, w_gate, w_branch, b_gate.reshape(nbr, 1, n))


def _xattn_kernel(h_ref, x_ref, wq_ref, kv_ref, wo_ref, gp_ref, gn_ref, xo_ref, ho_ref, *, heads, dh, qscale):
    q = (jnp.dot(h_ref[...], wq_ref[...], preferred_element_type=F32) * qscale).astype(BF16)
    kv = kv_ref[...]
    outs = []
    for hh in range(heads):
        k = kv[:, hh * dh:(hh + 1) * dh]
        v = kv[:, (heads + hh) * dh:(heads + hh + 1) * dh]
        s = lax.dot_general(q[:, hh * dh:(hh + 1) * dh], k, (((1,), (1,)), ((), ())),
                            preferred_element_type=F32)
        p = jnp.exp2(s - jnp.max(s, axis=-1, keepdims=True))
        o = jnp.dot(p.astype(BF16), v, preferred_element_type=F32) / jnp.sum(p, axis=-1, keepdims=True)
        outs.append(o.astype(BF16))
    y = jnp.dot(jnp.concatenate(outs, axis=1), wo_ref[...], preferred_element_type=F32)
    xn = x_ref[...] + _rms(y, gp_ref[...])
    xo_ref[...] = xn
    ho_ref[...] = _rms(xn, gn_ref[...]).astype(ho_ref.dtype)


def _cross_attention(h, x, kv, wq, wo, layer, g_post, g_next):
    m, d = x.shape
    dh, heads = HEAD_DIM, XA_HEADS
    n_mem = kv.shape[0]
    tm = _tile(m, 256)
    row = pl.BlockSpec((tm, d), lambda i: (i, 0))
    vec = pl.BlockSpec((1, d), lambda i: (0, 0))
    full = lambda shape: pl.BlockSpec(shape, lambda i: (0, 0))
    kern = functools.partial(_xattn_kernel, heads=heads, dh=dh, qscale=dh ** -0.5 * LOG2E)
    return pl.pallas_call(
        kern, grid=(m // tm,),
        in_specs=[row, row, pl.BlockSpec((None, d, heads * dh), lambda i: (layer, 0, 0)),
                  full((n_mem, 2 * heads * dh)), pl.BlockSpec((None, heads * dh, d), lambda i: (layer, 0, 0)), vec, vec],
        out_specs=[row, row],
        out_shape=[jax.ShapeDtypeStruct((m, d), F32), jax.ShapeDtypeStruct((m, d), BF16)],
        compiler_params=_cp(1), name="cross_attention")(
            h, x, wq, kv, wo, g_post.reshape(1, d), g_next.reshape(1, d))


def _rope_tables(seq):
    dh = HEAD_DIM
    n_rows = seq // GRID_W
    row = jnp.broadcast_to(jnp.arange(n_rows, dtype=F32)[:, None], (n_rows, GRID_W)).reshape(seq)
    col = jnp.broadcast_to(jnp.arange(GRID_W, dtype=F32)[None, :], (n_rows, GRID_W)).reshape(seq)
    axis_dim = dh // 2
    inv_freq = ROPE_THETA ** (-jnp.arange(0, axis_dim, 2, dtype=F32) / axis_dim)
    ang = jnp.stack([row[:, None] * inv_freq, col[:, None] * inv_freq], axis=1)
    ang = jnp.broadcast_to(ang[:, :, None, :], (seq, 2, 2, axis_dim // 2)).reshape(seq, dh)
    sign = jnp.where((jnp.arange(dh) % (dh // 2)) < dh // 4, -1.0, 1.0).astype(F32)
    return jnp.cos(ang), jnp.sin(ang) * sign[None, :]


def _hyena_tables(seq, width):
    bands = (HY_EMB - 1) // 2
    pos = jnp.arange(seq, dtype=F32)
    t = pos / (seq - 1)
    w = 2.0 * math.pi * pos / seq
    f = jnp.linspace(1e-4, bands - 1, bands, dtype=F32)
    ang = w[:, None] * f[None]
    feats = jnp.concatenate([t[:, None], jnp.cos(ang), -jnp.sin(ang)], axis=-1)
    deltas = jnp.abs(jnp.linspace(math.log(HY_DECAY_TARGET) / HY_SLOW_PCT,
                                  math.log(HY_DECAY_TARGET) / HY_FAST_PCT, width, dtype=F32))
    return feats, deltas


def kernel(x, mem, ffn1_pre_norm, ffn1_w1, ffn1_w3, ffn1_w2, ffn1_post_norm, mix_pre_norm, w_in, diff_lambda, diff_norm, gqa_q_norm, gqa_k_norm, ret_decay_logit, ret_norm, hy_conv_w, hy_conv_b, hy_w1, hy_b1, hy_w2, hy_b2, hy_w3, hy_sin_freq, hy_filter_bias, w_gate_up, b_gate, w_branch, w_out, mix_post_norm, xa_pre_norm, xa_mem_norm, xa_wq, xa_wkv, xa_wo, xa_post_norm, ffn2_pre_norm, ffn2_w1, ffn2_w3, ffn2_w2, ffn2_post_norm):
    batch, seq, d_model = x.shape
    assert batch == 1
    depth = w_in.shape[0]
    dh = HEAD_DIM
    unit = d_model // 4
    da_heads = unit // (2 * dh)
    gqa_heads = unit // dh
    gqa_kv = gqa_heads // 4
    ret_heads = unit // (2 * dh)
    hy_w = HY_ORDER + 1
    names = ("qa", "ka", "va", "qb", "kb", "vb", "qc", "kc", "vc", "gc", "hy", "gate")
    widths = (da_heads * 2 * dh, da_heads * 2 * dh, da_heads * 2 * dh, gqa_heads * dh, gqa_kv * dh,
              gqa_kv * dh, ret_heads * dh, ret_heads * dh, ret_heads * 2 * dh, ret_heads * 2 * dh,
              hy_w * unit, GATE_RANK)
    assert sum(widths) == w_in.shape[2]
    src = dict(zip(names, np.concatenate([[0], np.cumsum(widths)[:-1]]).tolist()))
    wid = dict(zip(names, widths))
    blk_w = dict(qa=2 * dh, ka=2 * dh, va=2 * dh, qb=wid["qb"], kb=wid["kb"], vb=wid["vb"], qc=dh, kc=dh,
                 vc=2 * dh, gc=2 * dh, gate=GATE_RANK)
    order_a = sorted(blk_w, key=lambda nm: -blk_w[nm])
    off, pos = {}, 0
    for nm in order_a:
        assert pos % blk_w[nm] == 0 and wid[nm] % blk_w[nm] == 0, nm
        off[nm] = pos
        pos += wid[nm]
    n_a = pos
    assert off["ka"] == off["qa"] + wid["qa"]

    att_scale = dh ** -0.5 * LOG2E
    colscale = jnp.ones((n_a,), F32)
    colscale = colscale.at[off["qa"]:off["qa"] + wid["qa"]].set(att_scale)
    colscale = colscale.at[off["kc"]:off["kc"] + wid["kc"]].set(dh ** -0.5)

    slopes = 2.0 ** (-8.0 * jnp.arange(1, da_heads + 1, dtype=F32) / da_heads)
    rope_cos, rope_sin = _rope_tables(seq)
    hy_feats, hy_deltas = _hyena_tables(seq, unit)
    tabs = _dft_tables(seq)

    ffn1_w = tuple(w.astype(BF16) for w in (ffn1_w1, ffn1_w3, ffn1_w2))
    ffn2_w = tuple(w.astype(BF16) for w in (ffn2_w1, ffn2_w3, ffn2_w2))
    w_gate_b, w_branch_b, w_out_b = w_gate_up.astype(BF16), w_branch.astype(BF16), w_out.astype(BF16)
    xa_wq_b, xa_wkv_b, xa_wo_b = xa_wq.astype(BF16), xa_wkv.astype(BF16), xa_wo.astype(BF16)

    xs = x.reshape(seq, d_model)
    mem2 = mem.reshape(mem.shape[1], d_model)
    h = _norm_cast(xs, ffn1_pre_norm[0])
    for l in range(depth):
        xs, h = _ffn(xs, h, *ffn1_w, l, ffn1_post_norm[l], mix_pre_norm[l])

        w_l = w_in[l]
        w_a = jnp.concatenate([w_l[:, src[nm]:src[nm] + wid[nm]] for nm in order_a], axis=1).astype(BF16)
        proj = _matmul(h, w_a, BF16, colscale)
        hy_u = _matmul(h, w_l[:, src["hy"]:src["gate"]].astype(BF16), F32)

        lam_init = 0.8 - 0.6 * math.exp(-0.3 * l)
        lp = diff_lambda[l]
        lam = jnp.exp(jnp.sum(lp[0] * lp[1])) - jnp.exp(jnp.sum(lp[2] * lp[3])) + lam_init
        scal = jnp.concatenate([lam.reshape(1), slopes * LOG2E]).astype(F32)
        top = _head_norm_max(proj, seq=seq, first_blk=off["qa"] // (2 * dh), n_blk=2 * da_heads)
        spread = 2.0 * jnp.max(jnp.sqrt(top[:da_heads] * top[da_heads:]), axis=1)
        reach = jnp.ceil((ALIBI_ZERO_LOG2 + spread) / (slopes * LOG2E))
        reach = jnp.where(reach < seq, jnp.maximum(reach, 0.0), seq).astype(jnp.int32)
        da_tq, da_tk = _tile(seq, DIFF_QUERY_BLOCK), _tile(seq // FLASH_SUBSTEPS, DIFF_KEY_BLOCK)
        schedule = _alibi_schedule(reach, seq=seq, tq=da_tq, span=FLASH_SUBSTEPS * da_tk)
        oa = _flash_attention(proj, proj, proj, seq=seq, heads=da_heads, n_maps=2, q_blk=off["qa"] // (2 * dh),
                              k_blk=off["ka"] // (2 * dh), v_blk=off["va"] // (2 * dh), tq=da_tq, tk=da_tk,
                              diff_args=(scal, diff_norm[l], 1.0 - lam_init, schedule))

        qg, kg, vg = _gqa_prep(proj, rope_cos, rope_sin, gqa_q_norm[l], gqa_k_norm[l], seq=seq,
                               q_w=wid["qb"], kv_w=wid["kb"], q_blk=off["qb"] // wid["qb"],
                               k_blk=off["kb"] // wid["kb"], v_blk=off["vb"] // wid["vb"], qscale=att_scale)
        ob = _flash_attention(qg, kg, vg, seq=seq, heads=gqa_kv, n_maps=gqa_heads // gqa_kv, q_blk=0, k_blk=0,
                              v_blk=0, tq=128, tk=GQA_KEY_BLOCK)

        log_g = -jax.nn.softplus(-ret_decay_logit[l].astype(F32))
        ret_args = dict(seq=seq, heads=ret_heads, q_blk=off["qc"] // dh, k_blk=off["kc"] // dh,
                        v_blk=off["vc"] // (2 * dh), gate_blk=off["gc"] // (2 * dh))
        y_bwd = _retention_pass(proj, _retention_tables(log_g[1], True), reverse=True, **ret_args)
        oc = _retention_pass(proj, _retention_tables(log_g[0], False), reverse=False, y_bwd=y_bwd,
                             gain=ret_norm[l], **ret_args)

        od = _hyena(hy_u, hy_conv_w[l], hy_conv_b[l],
                    (hy_w1[l], hy_b1[l], hy_w2[l], hy_b2[l], hy_w3[l], hy_sin_freq[l]),
                    hy_filter_bias[l], hy_feats, hy_deltas, tabs, seq=seq, width=unit)

        merged = _merge(proj, (oa, ob, oc, od), w_gate_b, b_gate[l], w_branch_b, l,
                        gate_blk=off["gate"] // GATE_RANK)
        y = _matmul(merged, w_out_b, BF16, layer=l)
        xs, h = _resid_norm(xs, y, mix_post_norm[l], 1.0, xa_pre_norm[l])

        mem_n = _norm_cast(mem2, xa_mem_norm[l])
        kv = _matmul(mem_n, xa_wkv_b, BF16, layer=l)
        xs, h = _cross_attention(h, xs, kv, xa_wq_b, xa_wo_b, l, xa_post_norm[l], ffn2_pre_norm[l])

        g_next = ffn1_pre_norm[l + 1] if l + 1 < depth else ffn2_pre_norm[l]
        xs, h = _ffn(xs, h, *ffn2_w, l, ffn2_post_norm[l], g_next)
    return xs.reshape(batch, seq, d_model)
```

```python
import functools
import math

import numpy as np
import jax
import jax.numpy as jnp
from jax import lax
from jax.experimental import pallas as pl
from jax.experimental.pallas import tpu as pltpu

F32 = jnp.float32
BF16 = jnp.bfloat16

EPS = 1e-6
HEAD_DIM = 128
GATE_RANK = 512
GRID_W = 64
ROPE_THETA = 10000.0
RET_CHUNK = 128
XA_HEADS = 4
HY_ORDER = 2
HY_EMB = 33
HY_DECAY_TARGET = 1e-2
HY_FAST_PCT = 0.3
HY_SLOW_PCT = 1.5
LOG2E = math.log2(math.e)
DFT_N2 = 256
NEG_BIG = -1e30

VMEM_LIMIT_BYTES = 56 * 1024 * 1024


def _cp(n_axes):
    return pltpu.CompilerParams(dimension_semantics=("arbitrary",) * n_axes,
                                vmem_limit_bytes=VMEM_LIMIT_BYTES)


def _tile(n, pref, align=128):
    for t in range(min(n, pref), 0, -1):
        if n % t == 0 and t % align == 0:
            return t
    raise ValueError((n, pref, align))


def _rms(x, gain):
    return x * lax.rsqrt(jnp.mean(x * x, axis=-1, keepdims=True) + EPS) * gain


def _norm_kernel(x_ref, g_ref, h_ref):
    h_ref[...] = _rms(x_ref[...], g_ref[...]).astype(h_ref.dtype)


def _norm_cast(x, gain):
    m, d = x.shape
    tm = _tile(m, 256)
    row = pl.BlockSpec((tm, d), lambda i: (i, 0))
    vec = pl.BlockSpec((1, d), lambda i: (0, 0))
    return pl.pallas_call(
        _norm_kernel, grid=(m // tm,), in_specs=[row, vec], out_specs=row,
        out_shape=jax.ShapeDtypeStruct((m, d), BF16), compiler_params=_cp(1),
        name="norm_cast")(x, gain.reshape(1, d))


def _resid_norm_kernel(x_ref, y_ref, gp_ref, gn_ref, xo_ref, ho_ref, *, alpha):
    xn = x_ref[...] + alpha * _rms(y_ref[...].astype(F32), gp_ref[...])
    xo_ref[...] = xn
    ho_ref[...] = _rms(xn, gn_ref[...]).astype(ho_ref.dtype)


def _resid_norm(x, y, g_post, alpha, g_next):
    m, d = x.shape
    tm = _tile(m, 256)
    row = pl.BlockSpec((tm, d), lambda i: (i, 0))
    vec = pl.BlockSpec((1, d), lambda i: (0, 0))
    return pl.pallas_call(
        functools.partial(_resid_norm_kernel, alpha=alpha), grid=(m // tm,),
        in_specs=[row, row, vec, vec], out_specs=[row, row],
        out_shape=[jax.ShapeDtypeStruct((m, d), F32), jax.ShapeDtypeStruct((m, d), BF16)],
        compiler_params=_cp(1), name="resid_norm")(x, y, g_post.reshape(1, d), g_next.reshape(1, d))


def _mm_kernel(a_ref, w_ref, s_ref, o_ref):
    acc = jnp.dot(a_ref[...], w_ref[...], preferred_element_type=F32)
    o_ref[...] = (acc * s_ref[...]).astype(o_ref.dtype)


def _matmul(a, w, out_dtype, colscale=None, layer=None, tm=1024, tn=1024):
    m, k = a.shape
    n = w.shape[-1]
    tm, tn = _tile(m, tm), _tile(n, tn)
    if colscale is None:
        colscale = jnp.ones((n,), F32)
    if layer is None:
        w_spec = pl.BlockSpec((k, tn), lambda j, i: (0, j))
    else:
        w_spec = pl.BlockSpec((None, k, tn), lambda j, i: (layer, 0, j))
    return pl.pallas_call(
        _mm_kernel, grid=(n // tn, m // tm),
        in_specs=[pl.BlockSpec((tm, k), lambda j, i: (i, 0)), w_spec, pl.BlockSpec((1, tn), lambda j, i: (0, j))],
        out_specs=pl.BlockSpec((tm, tn), lambda j, i: (i, j)),
        out_shape=jax.ShapeDtypeStruct((m, n), out_dtype),
        compiler_params=_cp(2), name="matmul")(a, w, colscale.reshape(1, n).astype(F32))


def _ffn_up_kernel(h_ref, w1_ref, w3_ref, o_ref):
    h = h_ref[...]
    a = jnp.dot(h, w1_ref[...], preferred_element_type=F32)
    b = jnp.dot(h, w3_ref[...], preferred_element_type=F32)
    o_ref[...] = (a * jax.nn.sigmoid(a) * b).astype(o_ref.dtype)


def _ffn_up(h, w1, w3, layer, tm=1024, tn=512):
    m, k = h.shape
    n = w1.shape[-1]
    tm, tn = _tile(m, tm), _tile(n, tn)
    wspec = pl.BlockSpec((None, k, tn), lambda j, i: (layer, 0, j))
    return pl.pallas_call(
        _ffn_up_kernel, grid=(n // tn, m // tm),
        in_specs=[pl.BlockSpec((tm, k), lambda j, i: (i, 0)), wspec, wspec],
        out_specs=pl.BlockSpec((tm, tn), lambda j, i: (i, j)),
        out_shape=jax.ShapeDtypeStruct((m, n), BF16),
        compiler_params=_cp(2), name="ffn_up")(h, w1, w3)


def _ffn(x, h, w1, w3, w2, layer, g_post, g_next):
    u = _ffn_up(h, w1, w3, layer)
    y = _matmul(u, w2, BF16, layer=layer)
    return _resid_norm(x, y, g_post, 0.5, g_next)


FLASH_SUBSTEPS = 2
DIFF_QUERY_BLOCK, DIFF_KEY_BLOCK, GQA_KEY_BLOCK = 256, 1024, 2048
ALIBI_ZERO_LOG2 = 150.0


def _alibi_mixed_bases(tq, tk):
    step = math.gcd(tq, tk)
    return [b for b in range(-(tq // step) * step, tk + step, step) if 1 - tq < b < tk - 1], step


def _flash_kernel(*refs, n_maps, dh, tq, tk, nq, pairs, diff, post_scale):
    if diff:
        (n_ref, qb_ref, pr_ref, fst_ref, lst_ref, scal_ref, q_ref, k_ref, v_ref, g_ref, o_ref,
         s_buf, p_buf, al_buf, m_ref, acc_ref, l_ref, tbl_ref) = refs
        mixed, base_step = _alibi_mixed_bases(tq, tk)
    else:
        q_ref, k_ref, v_ref, o_ref, s_buf, p_buf, al_buf, m_ref, acc_ref = refs
    h, g = pl.program_id(0), pl.program_id(1)
    n_sub = FLASH_SUBSTEPS
    if diff:
        n_real = n_ref[h]

        def entry(e):
            e = jnp.maximum(e, 0)
            return qb_ref[h, e], pr_ref[h, e], fst_ref[h, e] == 1, lst_ref[h, e] == 1
    else:
        n_real = nq * pairs

        def entry(e):
            e = jnp.clip(e, 0, n_real - 1)
            return e // pairs, e % pairs, e % pairs == 0, e % pairs == pairs - 1

    @pl.when(g == 0)
    def _():
        s_buf[...] = jnp.zeros(s_buf.shape, F32)
        p_buf[...] = jnp.zeros(p_buf.shape, BF16)
        al_buf[...] = jnp.ones(al_buf.shape, F32)
        m_ref[...] = jnp.full(m_ref.shape, NEG_BIG, F32)
        acc_ref[...] = jnp.zeros(acc_ref.shape, F32)
        if diff:
            l_ref[...] = jnp.zeros(l_ref.shape, F32)
            d = lax.broadcasted_iota(jnp.int32, (tq, tk), 0) - lax.broadcasted_iota(jnp.int32, (tq, tk), 1)
            slope = scal_ref[1 + h]
            tbl_ref[0] = d.astype(F32) * slope
            tbl_ref[1] = (-d).astype(F32) * slope
            for i, mb in enumerate(mixed):
                tbl_ref[2 + i] = jnp.abs(d + mb).astype(F32) * slope

    def pipeline_step():
        for sub in range(n_sub):
            other = 1 - sub
            for c in range(n_maps):
                k_lo = c * dh if diff else 0
                s_buf[sub, c] = lax.dot_general(q_ref[:, c * dh:(c + 1) * dh],
                                                k_ref[sub * tk:(sub + 1) * tk, k_lo:k_lo + dh],
                                                (((1,), (1,)), ((), ())), preferred_element_type=F32)
            e = g - 1 + sub
            valid = jnp.logical_and(e >= 0, e < n_real)
            qb, pr, fst, _ = entry(e)
            kb = n_sub * pr + other
            par = qb % 2
            if diff:
                base = qb * tq - kb * tk
                below, above = base >= tk - 1, base <= 1 - tq
                sel = jnp.where(below, 0, jnp.where(above, 1, 2 + (base - mixed[0]) // base_step))
                bias = tbl_ref[jnp.clip(sel, 0, 1 + len(mixed))]
                basef = (jnp.zeros((tq, 1), jnp.int32) + base).astype(F32) * scal_ref[1 + h]
                cst = jnp.where(below, -basef, jnp.where(above, basef, 0.0))
            else:
                cst = jnp.zeros((tq, 1), F32)
            cst = jnp.where(valid, cst, -jnp.inf)
            for c in range(n_maps):
                s = s_buf[other, c]
                if diff:
                    s = s - bias
                m_prev = m_ref[par, c]
                if other == 0:
                    m_prev = jnp.where(fst, NEG_BIG, m_prev)
                m_new = jnp.maximum(m_prev, jnp.max(s, axis=-1, keepdims=True) + cst)
                alpha = jnp.exp2(m_prev - m_new)
                p = jnp.exp2(s - (m_new - cst))
                if diff:
                    l_ref[par, c] = alpha * l_ref[par, c] + jnp.sum(p, axis=-1, keepdims=True)
                m_ref[par, c] = m_new
                al_buf[other, c] = alpha
                p_buf[other, c] = p.astype(BF16)
            v = v_ref[sub * tk:(sub + 1) * tk, :]
            for c in range(n_maps):
                acc_ref[c] = al_buf[sub, c] * acc_ref[c] + jnp.dot(p_buf[sub, c], v, preferred_element_type=F32)

    if diff:
        pl.when(g <= n_real)(pipeline_step)
    else:
        pipeline_step()

    qb_w, _, _, last_w = entry(g - 1)

    @pl.when(jnp.logical_and(jnp.logical_and(g >= 1, g <= n_real), last_w))
    def _():
        if diff:
            par_w = qb_w % 2
            o = acc_ref[0] / l_ref[par_w, 0] - scal_ref[0] * (acc_ref[1] / l_ref[par_w, 1])
            o_ref[...] = (_rms(o, g_ref[...]) * post_scale).astype(o_ref.dtype)
        else:
            for r in range(n_maps):
                a = acc_ref[r]
                o_ref[:, r * dh:(r + 1) * dh] = (a[:, :dh] / a[:, dh:]).astype(o_ref.dtype)


def _alibi_schedule(reach, *, seq, tq, span):
    nq, pairs = seq // tq, seq // span
    q_lo = jnp.arange(nq, dtype=jnp.int32) * tq
    lo = jnp.clip((q_lo[None, :] - reach[:, None]) // span, 0, pairs - 1)
    hi = jnp.clip((q_lo[None, :] + (tq - 1) + reach[:, None]) // span, 0, pairs - 1)
    cnt = hi - lo + 1
    ends = jnp.cumsum(cnt, axis=1)
    starts = ends - cnt
    n_real = ends[:, -1]
    e = jnp.minimum(jnp.arange(nq * pairs + 1, dtype=jnp.int32)[None, :], n_real[:, None] - 1)
    qb = jnp.sum((ends[:, None, :] <= e[:, :, None]).astype(jnp.int32), axis=2)
    pick = qb[:, :, None] == jnp.arange(nq, dtype=jnp.int32)[None, None, :]
    st = jnp.sum(jnp.where(pick, starts[:, None, :], 0), axis=2)
    pr = jnp.sum(jnp.where(pick, lo[:, None, :], 0), axis=2) + e - st
    fst = (e == st).astype(jnp.int32)
    lst = (e == jnp.sum(jnp.where(pick, ends[:, None, :], 0), axis=2) - 1).astype(jnp.int32)
    return n_real.astype(jnp.int32), qb, pr.astype(jnp.int32), fst, lst


def _flash_attention(q, k, v, *, seq, heads, n_maps, q_blk, k_blk, v_blk, tq, tk, diff_args=None):
    dh = HEAD_DIM
    n_sub = FLASH_SUBSTEPS
    diff = diff_args is not None
    tq = _tile(seq, tq)
    tk = _tile(seq // n_sub, tk)
    nq, pairs = seq // tq, seq // (n_sub * tk)
    kern = functools.partial(_flash_kernel, n_maps=n_maps, dh=dh, tq=tq, tk=tk, nq=nq, pairs=pairs, diff=diff,
                             post_scale=diff_args[2] if diff else None)
    scratch = [pltpu.VMEM((n_sub, n_maps, tq, tk), F32), pltpu.VMEM((n_sub, n_maps, tq, tk), BF16),
               pltpu.VMEM((n_sub, n_maps, tq, 1), F32), pltpu.VMEM((2, n_maps, tq, 1), F32),
               pltpu.VMEM((n_maps, tq, 2 * dh), F32)]
    out_shape = jax.ShapeDtypeStruct((seq, heads * n_maps * dh), BF16)
    if not diff:
        last = nq * pairs - 1
        return pl.pallas_call(
            kern, grid=(heads, nq * pairs + 1),
            in_specs=[pl.BlockSpec((tq, n_maps * dh), lambda h, g: (jnp.minimum(g, last) // pairs, q_blk + h)),
                      pl.BlockSpec((n_sub * tk, dh), lambda h, g: (jnp.minimum(g, last) % pairs, k_blk + h)),
                      pl.BlockSpec((n_sub * tk, 2 * dh), lambda h, g: (jnp.maximum(g - 1, 0) % pairs, v_blk + h))],
            out_specs=pl.BlockSpec((tq, n_maps * dh), lambda h, g: (jnp.maximum(g - 1, 0) // pairs, h)),
            out_shape=out_shape, scratch_shapes=scratch, compiler_params=_cp(2), name="gqa_attention")(q, k, v)
    scal, gain, _, schedule = diff_args
    scratch += [pltpu.VMEM((2, n_maps, tq, 1), F32),
                pltpu.VMEM((2 + len(_alibi_mixed_bases(tq, tk)[0]), tq, tk), F32)]
    prev = lambda g: jnp.maximum(g - 1, 0)
    grid_spec = pltpu.PrefetchScalarGridSpec(
        num_scalar_prefetch=5, grid=(heads, nq * pairs + 1),
        in_specs=[pl.BlockSpec(memory_space=pltpu.SMEM),
                  pl.BlockSpec((tq, n_maps * dh), lambda h, g, n, qb, pr, fs, ls: (qb[h, g], q_blk + h)),
                  pl.BlockSpec((n_sub * tk, n_maps * dh), lambda h, g, n, qb, pr, fs, ls: (pr[h, g], k_blk + h)),
                  pl.BlockSpec((n_sub * tk, 2 * dh), lambda h, g, n, qb, pr, fs, ls: (pr[h, prev(g)], v_blk + h)),
                  pl.BlockSpec((1, 2 * dh), lambda h, g, n, qb, pr, fs, ls: (0, 0))],
        out_specs=pl.BlockSpec((tq, n_maps * dh), lambda h, g, n, qb, pr, fs, ls: (qb[h, prev(g)], h)),
        scratch_shapes=scratch)
    return pl.pallas_call(kern, grid_spec=grid_spec, out_shape=out_shape, compiler_params=_cp(2),
                          name="diff_attention")(*schedule, scal, q, k, v, gain.reshape(1, 2 * dh))


def _norm_max_kernel(x_ref, o_ref, *, dh):
    @pl.when(pl.program_id(1) == 0)
    def _():
        o_ref[...] = jnp.zeros(o_ref.shape, F32)

    x = x_ref[...].astype(F32)
    sq = x * x
    for c in range(x.shape[1] // dh):
        top = jnp.max(jnp.sum(sq[:, c * dh:(c + 1) * dh], axis=1, keepdims=True), axis=0, keepdims=True)
        o_ref[0, :, c * dh:(c + 1) * dh] = jnp.maximum(o_ref[0, :, c * dh:(c + 1) * dh], top)


def _head_norm_max(proj, *, seq, first_blk, n_blk):
    dh = HEAD_DIM
    tm = _tile(seq, 2048)
    out = pl.pallas_call(
        functools.partial(_norm_max_kernel, dh=dh), grid=(n_blk, seq // tm),
        in_specs=[pl.BlockSpec((tm, 2 * dh), lambda b, i: (i, first_blk + b))],
        out_specs=pl.BlockSpec((1, 8, 2 * dh), lambda b, i: (b, 0, 0)),
        out_shape=jax.ShapeDtypeStruct((n_blk, 8, 2 * dh), F32),
        compiler_params=_cp(2), name="head_norm_max")(proj)
    return out[:, 0, ::dh]


def _gqa_prep_kernel(q_ref, k_ref, v_ref, cos_ref, sin_ref, gq_ref, gk_ref, qo_ref, ko_ref, vo_ref, *, qscale):
    cos = cos_ref[...]
    sin = sin_ref[...]
    dh = cos.shape[-1]
    lane = lax.broadcasted_iota(jnp.int32, cos.shape, 1)
    first_half = (lane % (dh // 2)) < (dh // 4)

    def norm_rope(xh, gain):
        xn = _rms(xh.astype(F32), gain)
        rot = jnp.where(first_half, pltpu.roll(xn, dh - dh // 4, 1), pltpu.roll(xn, dh // 4, 1))
        return xn * cos + rot * sin

    for hh in range(q_ref.shape[1] // dh):
        sl = slice(hh * dh, (hh + 1) * dh)
        qo_ref[:, sl] = (norm_rope(q_ref[:, sl], gq_ref[...]) * qscale).astype(qo_ref.dtype)
    for hh in range(k_ref.shape[1] // dh):
        sl = slice(hh * dh, (hh + 1) * dh)
        ko_ref[:, sl] = norm_rope(k_ref[:, sl], gk_ref[...]).astype(ko_ref.dtype)
        vo_ref[:, 2 * hh * dh:(2 * hh + 1) * dh] = v_ref[:, sl]
        vo_ref[:, (2 * hh + 1) * dh:(2 * hh + 2) * dh] = jnp.ones((v_ref.shape[0], dh), vo_ref.dtype)


def _gqa_prep(proj, cos, sin_signed, gq, gk, *, seq, q_w, kv_w, q_blk, k_blk, v_blk, qscale):
    dh = HEAD_DIM
    tm = _tile(seq, 512)
    vec = pl.BlockSpec((1, dh), lambda i: (0, 0))
    tab = pl.BlockSpec((tm, dh), lambda i: (i, 0))
    return pl.pallas_call(
        functools.partial(_gqa_prep_kernel, qscale=qscale), grid=(seq // tm,),
        in_specs=[pl.BlockSpec((tm, q_w), lambda i: (i, q_blk)),
                  pl.BlockSpec((tm, kv_w), lambda i: (i, k_blk)),
                  pl.BlockSpec((tm, kv_w), lambda i: (i, v_blk)), tab, tab, vec, vec],
        out_specs=[pl.BlockSpec((tm, q_w), lambda i: (i, 0)), pl.BlockSpec((tm, kv_w), lambda i: (i, 0)),
                   pl.BlockSpec((tm, 2 * kv_w), lambda i: (i, 0))],
        out_shape=[jax.ShapeDtypeStruct((seq, q_w), BF16), jax.ShapeDtypeStruct((seq, kv_w), BF16),
                   jax.ShapeDtypeStruct((seq, 2 * kv_w), BF16)],
        compiler_params=_cp(1), name="gqa_prep")(proj, proj, proj, cos, sin_signed, gq.reshape(1, dh),
                                                gk.reshape(1, dh))


def _ret_kernel(gch_ref, q_ref, k_ref, v_ref, dm_ref, xi_ref, ze_ref, *rest, chunk, n_sub, reverse, final):
    if final:
        yb_ref, gate_ref, gain_ref, o_ref, st_ref = rest
    else:
        o_ref, st_ref = rest
    h, n = pl.program_id(0), pl.program_id(1)

    @pl.when(n == 0)
    def _():
        st_ref[...] = jnp.zeros(st_ref.shape, F32)

    dm, xi, ze = dm_ref[0], xi_ref[0], ze_ref[0]
    g_chunk = gch_ref[h]
    for c in (range(n_sub - 1, -1, -1) if reverse else range(n_sub)):
        sl = slice(c * chunk, (c + 1) * chunk)
        q, k, v = q_ref[sl, :], k_ref[sl, :], v_ref[sl, :]
        s = lax.dot_general(q, k, (((1,), (1,)), ((), ())), preferred_element_type=F32) * dm
        st = st_ref[...]
        y = jnp.dot(s.astype(BF16), v, preferred_element_type=F32)
        y = y + jnp.dot((q.astype(F32) * xi).astype(BF16), st.astype(BF16), preferred_element_type=F32)
        kz = (k.astype(F32) * ze).astype(BF16)
        st_ref[...] = g_chunk * st + lax.dot_general(kz, v, (((0,), (0,)), ((), ())),
                                                     preferred_element_type=F32)
        if final:
            gate = gate_ref[sl, :].astype(F32)
            y = _rms(y + yb_ref[sl, :], gain_ref[...]) * (gate * jax.nn.sigmoid(gate))
        o_ref[sl, :] = y.astype(o_ref.dtype)


def _retention_pass(proj, tables, *, seq, heads, q_blk, k_blk, v_blk, gate_blk, reverse, y_bwd=None,
                    gain=None):
    dk, dv, chunk = HEAD_DIM, 2 * HEAD_DIM, RET_CHUNK
    g_chunk, dmask, xi, zeta = tables
    rows = _tile(seq, 8 * chunk)
    n_steps = seq // rows
    final = y_bwd is not None
    blk = (lambda n: n_steps - 1 - n) if reverse else (lambda n: n)
    head_tab = lambda shape: pl.BlockSpec((1,) + shape, lambda h, n: (h, 0, 0))
    in_specs = [pl.BlockSpec(memory_space=pltpu.SMEM),
                pl.BlockSpec((rows, dk), lambda h, n: (blk(n), q_blk + h)),
                pl.BlockSpec((rows, dk), lambda h, n: (blk(n), k_blk + h)),
                pl.BlockSpec((rows, dv), lambda h, n: (blk(n), v_blk + h)),
                head_tab((chunk, chunk)), head_tab((chunk, 1)), head_tab((chunk, 1))]
    args = [g_chunk, proj, proj, proj, dmask, xi, zeta]
    if final:
        in_specs += [pl.BlockSpec((rows, dv), lambda h, n: (blk(n), h)),
                     pl.BlockSpec((rows, dv), lambda h, n: (blk(n), gate_blk + h)),
                     pl.BlockSpec((1, dv), lambda h, n: (0, 0))]
        args += [y_bwd, proj, gain.reshape(1, dv)]
    kern = functools.partial(_ret_kernel, chunk=chunk, n_sub=rows // chunk, reverse=reverse, final=final)
    return pl.pallas_call(
        kern, grid=(heads, n_steps), in_specs=in_specs,
        out_specs=pl.BlockSpec((rows, dv), lambda h, n: (blk(n), h)),
        out_shape=jax.ShapeDtypeStruct((seq, heads * dv), BF16 if final else F32),
        scratch_shapes=[pltpu.VMEM((dk, dv), F32)],
        compiler_params=_cp(2), name="retention")(*args)


def _retention_tables(log_g, reverse):
    c = RET_CHUNK
    idx = jnp.arange(c, dtype=F32)
    diff = idx[:, None] - idx[None, :]
    if reverse:
        diff = -diff
        mask = diff > 0
        xi = jnp.exp((c - idx)[None, :] * log_g[:, None])
        zeta = jnp.exp(idx[None, :] * log_g[:, None])
    else:
        mask = diff >= 0
        xi = jnp.exp((idx + 1.0)[None, :] * log_g[:, None])
        zeta = jnp.exp((c - 1 - idx)[None, :] * log_g[:, None])
    dmask = jnp.where(mask[None], jnp.exp(jnp.where(mask, diff, 0.0)[None] * log_g[:, None, None]), 0.0)
    return jnp.exp(c * log_g), dmask, xi[:, :, None], zeta[:, :, None]


def _dft_tables(seq):
    n = 2 * seq
    n2 = DFT_N2
    n1 = n // n2
    h1 = n1 // 2
    nk1 = n1 // 2 + 1
    k1p = -(-nk1 // 16) * 16
    a2 = np.arange(n2)[:, None, None]
    k1 = np.arange(nk1)[None, :, None]
    a1 = np.arange(h1)[None, None, :]
    theta = 2.0 * np.pi * ((a1 * k1 % n1) / n1 + (a2 * k1 % n) / n)
    fwd = np.zeros((n2, 2 * k1p, h1))
    fwd[:, :nk1] = np.cos(theta)
    fwd[:, k1p:k1p + nk1] = -np.sin(theta)
    weight = np.where((np.arange(nk1) == 0) | (np.arange(nk1) == n1 // 2), 1.0, 2.0)[None, :, None]
    inv = np.zeros((n2, h1, 2 * k1p))
    inv[:, :, :nk1] = np.transpose(weight * np.cos(theta), (0, 2, 1))
    inv[:, :, k1p:k1p + nk1] = np.transpose(-weight * np.sin(theta), (0, 2, 1))
    ang = 2.0 * np.pi * (np.arange(n2)[:, None] * np.arange(n2)[None, :] % n2) / n2
    c, s = np.cos(ang), np.sin(ang)
    f_blk = np.block([[c, s], [-s, c]])
    fi_blk = np.block([[c, -s], [s, c]])
    as_bf16 = lambda a: jnp.asarray(a.astype(BF16))
    return dict(n=n, n1=n1, h1=h1, nk1=nk1, k1p=k1p, fwd=as_bf16(fwd), inv=as_bf16(inv),
                f_blk=as_bf16(f_blk), fi_blk=as_bf16(fi_blk))


def _hy_filter_kernel(feat_ref, w1_ref, b1_ref, w2_ref, b2_ref, w3_ref, fr_ref, dl_ref, h_ref, l1_ref,
                      *, width, h1):
    i = pl.program_id(0)
    feats = feat_ref[...]
    fr = fr_ref[...]
    hid = jnp.sin(fr * (jnp.dot(feats.astype(BF16), w1_ref[...], preferred_element_type=F32) + b1_ref[...]))
    hid = jnp.sin(fr * (jnp.dot(hid.astype(BF16), w2_ref[...], preferred_element_type=F32) + b2_ref[...]))
    h = jnp.dot(hid.astype(BF16), w3_ref[...], preferred_element_type=F32)
    window = jnp.exp(-feats[:, 0:1] * dl_ref[...])
    n_groups = h.shape[1] // width
    row = lax.broadcasted_iota(jnp.int32, (h.shape[0], width), 0)
    not_lag0 = jnp.logical_or(row > 0, i > 0)

    @pl.when(i == 0)
    def _():
        l1_ref[...] = jnp.zeros(l1_ref.shape, F32)

    for g in range(n_groups):
        sl = slice(g * width, (g + 1) * width)
        hg = h[:, sl] * window
        if g >= n_groups // 2:
            hg = jnp.where(not_lag0, hg, 0.0)
        l1_ref[:, sl] += jnp.sum(jnp.abs(hg), axis=0, keepdims=True)
        hg = hg.astype(h_ref.dtype)
        for s in range(h.shape[0] // h1):
            lo = s * h.shape[1] + g * width
            h_ref[:, lo:lo + width] = hg[s * h1:(s + 1) * h1, :]


def _hy_filters(feats, deltas, w1, b1, w2, b2, w3, freq, *, seq, width, h1):
    hidden = w1.shape[1]
    emb = -(-w1.shape[0] // 128) * 128
    feats = jnp.pad(feats, ((0, 0), (0, emb - feats.shape[1])))
    feats = feats.reshape(h1, DFT_N2, emb).transpose(1, 0, 2).reshape(seq, emb)
    w1 = jnp.pad(w1, ((0, emb - w1.shape[0]), (0, 0)))
    cols = w3.shape[1]
    tl = max(_tile(seq, 512), h1)
    n_sub = tl // h1
    full = lambda shape: pl.BlockSpec(shape, lambda i: (0,) * len(shape))
    return pl.pallas_call(
        functools.partial(_hy_filter_kernel, width=width, h1=h1), grid=(seq // tl,),
        in_specs=[pl.BlockSpec((tl, emb), lambda i: (i, 0)), full((emb, hidden)), full((1, hidden)),
                  full((hidden, hidden)), full((1, hidden)), full((hidden, cols)), full((1, hidden)),
                  full((1, width))],
        out_specs=[pl.BlockSpec((h1, n_sub * cols), lambda i: (0, i)), full((1, cols))],
        out_shape=[jax.ShapeDtypeStruct((h1, DFT_N2 * cols), BF16), jax.ShapeDtypeStruct((1, cols), F32)],
        compiler_params=_cp(1), name="hy_filters")(
            feats, w1.astype(BF16), b1.reshape(1, hidden), w2.astype(BF16), b2.reshape(1, hidden),
            w3.astype(BF16), freq.reshape(1, hidden), deltas.reshape(1, width))


def _hy_short_conv_kernel(u_ref, up_ref, un_ref, w_ref, b_ref, x0_ref, x1_ref, v_ref, vb_ref):
    i = pl.program_id(0)
    u = u_ref[...]
    tm = u.shape[0]
    row = lax.broadcasted_iota(jnp.int32, u.shape, 0)
    prev_row = jnp.where(i > 0, up_ref[7:8, :], 0.0)
    next_row = jnp.where(i < pl.num_programs(0) - 1, un_ref[0:1, :], 0.0)
    before = jnp.where(row == 0, prev_row, pltpu.roll(u, 1, 0))
    after = jnp.where(row == tm - 1, next_row, pltpu.roll(u, tm - 1, 0))
    y = before * w_ref[0:1, :] + u * w_ref[1:2, :] + after * w_ref[2:3, :] + b_ref[...]
    wd = x0_ref.shape[1]
    x0_ref[...] = y[:, :wd].astype(x0_ref.dtype)
    x1_ref[...] = y[:, wd:2 * wd].astype(x1_ref.dtype)
    v_ref[...] = y[:, 2 * wd:]
    vb_ref[...] = y[:, 2 * wd:].astype(vb_ref.dtype)


def _hy_short_conv(u, w, b, *, seq, width):
    cols = u.shape[1]
    tm = _tile(seq, 256)
    nb8 = tm // 8
    last8 = seq // 8 - 1
    out = pl.BlockSpec((tm, width), lambda i: (i, 0))
    return pl.pallas_call(
        _hy_short_conv_kernel, grid=(seq // tm,),
        in_specs=[pl.BlockSpec((tm, cols), lambda i: (i, 0)),
                  pl.BlockSpec((8, cols), lambda i: (jnp.maximum(i * nb8 - 1, 0), 0)),
                  pl.BlockSpec((8, cols), lambda i: (jnp.minimum((i + 1) * nb8, last8), 0)),
                  pl.BlockSpec((3, cols), lambda i: (0, 0)), pl.BlockSpec((1, cols), lambda i: (0, 0))],
        out_specs=[out, out, out, out],
        out_shape=[jax.ShapeDtypeStruct((seq, width), dt) for dt in (BF16, BF16, F32, BF16)],
        compiler_params=_cp(1), name="hy_short_conv")(u, u, u, w, b.reshape(1, cols))


def _dft_a_kernel(x_ref, g_ref, re_ref, im_ref, *, n_sub, cols, k1p):
    for s in range(n_sub):
        sl = slice(s * cols, (s + 1) * cols)
        t = jnp.dot(g_ref[s], x_ref[:, sl], preferred_element_type=F32)
        re_ref[:, sl] = t[:k1p].astype(re_ref.dtype)
        im_ref[:, sl] = t[k1p:].astype(im_ref.dtype)


def _dft_a(x, tabs, *, cols):
    h1, k1p = tabs["h1"], tabs["k1p"]
    xv = x.reshape(h1, DFT_N2 * cols)
    n_sub = max(1, min(DFT_N2, 8192 // cols))
    spec_o = pl.BlockSpec((k1p, n_sub * cols), lambda i: (0, i))
    out = jax.ShapeDtypeStruct((k1p, DFT_N2 * cols), BF16)
    re, im = pl.pallas_call(
        functools.partial(_dft_a_kernel, n_sub=n_sub, cols=cols, k1p=k1p), grid=(DFT_N2 // n_sub,),
        in_specs=[pl.BlockSpec((h1, n_sub * cols), lambda i: (0, i)),
                  pl.BlockSpec((n_sub, 2 * k1p, h1), lambda i: (i, 0, 0))],
        out_specs=[spec_o, spec_o], out_shape=[out, out],
        compiler_params=_cp(1), name="dft_outer")(xv, tabs["fwd"])
    return re.reshape(k1p, DFT_N2, cols), im.reshape(k1p, DFT_N2, cols)


def _hy_filter_spec_kernel(fr_ref, fi_ref, br_ref, bi_ref, f_ref, w_ref, kr_ref, ki_ref):
    n2 = fr_ref.shape[1]
    f = f_ref[...]
    sf = jnp.dot(f, jnp.concatenate([fr_ref[0], fi_ref[0]], axis=0), preferred_element_type=F32)
    sb = jnp.dot(f, jnp.concatenate([br_ref[0], bi_ref[0]], axis=0), preferred_element_type=F32)
    w = w_ref[...]
    kr_ref[0] = ((sf[:n2] + sb[:n2]) * w).astype(kr_ref.dtype)
    ki_ref[0] = ((sf[n2:] - sb[n2:]) * w).astype(ki_ref.dtype)


def _hy_filter_spectrum(h_re, h_im, tabs, wnorm, *, cols):
    nk1 = tabs["nk1"]
    tc = _tile(cols, 1024)
    nb = cols // tc
    fwd = pl.BlockSpec((1, DFT_N2, tc), lambda k, j: (k, 0, j))
    bwd = pl.BlockSpec((1, DFT_N2, tc), lambda k, j: (k, 0, nb + j))
    out = jax.ShapeDtypeStruct((nk1, DFT_N2, cols), BF16)
    return pl.pallas_call(
        _hy_filter_spec_kernel, grid=(nk1, nb),
        in_specs=[fwd, fwd, bwd, bwd, pl.BlockSpec((2 * DFT_N2, 2 * DFT_N2), lambda k, j: (0, 0)),
                  pl.BlockSpec((1, tc), lambda k, j: (0, j))],
        out_specs=[fwd, fwd], out_shape=[out, out],
        compiler_params=_cp(2), name="hy_filter_spectrum")(h_re, h_im, h_re, h_im, tabs["f_blk"], wnorm)


def _hy_conv_kernel(tr_ref, ti_ref, kr_ref, ki_ref, f_ref, fi_ref, cr_ref, ci_ref, *, nk1):
    k1 = pl.program_id(0)
    n2 = tr_ref.shape[1]

    @pl.when(k1 < nk1)
    def _():
        s = jnp.dot(f_ref[...], jnp.concatenate([tr_ref[0], ti_ref[0]], axis=0), preferred_element_type=F32)
        sr, si = s[:n2], s[n2:]
        kr, ki = kr_ref[0].astype(F32), ki_ref[0].astype(F32)
        y = jnp.concatenate([sr * kr - si * ki, sr * ki + si * kr], axis=0).astype(BF16)
        c = jnp.dot(fi_ref[...], y, preferred_element_type=F32)
        cr_ref[0] = c[:n2].astype(cr_ref.dtype)
        ci_ref[0] = c[n2:].astype(ci_ref.dtype)

    @pl.when(k1 >= nk1)
    def _():
        cr_ref[...] = jnp.zeros(cr_ref.shape, cr_ref.dtype)
        ci_ref[...] = jnp.zeros(ci_ref.shape, ci_ref.dtype)


def _hy_conv(t_re, t_im, k_re, k_im, tabs, *, order, cols):
    nk1, k1p = tabs["nk1"], tabs["k1p"]
    tc = _tile(cols, 1024)
    nb = cols // tc
    sig = pl.BlockSpec((1, DFT_N2, tc), lambda k, j: (k, 0, j))
    flt = pl.BlockSpec((1, DFT_N2, tc), lambda k, j: (jnp.minimum(k, nk1 - 1), 0, order * nb + j))
    mat = pl.BlockSpec((2 * DFT_N2, 2 * DFT_N2), lambda k, j: (0, 0))
    out = jax.ShapeDtypeStruct((k1p, DFT_N2, cols), BF16)
    return pl.pallas_call(
        functools.partial(_hy_conv_kernel, nk1=nk1), grid=(k1p, nb),
        in_specs=[sig, sig, flt, flt, mat, mat], out_specs=[sig, sig], out_shape=[out, out],
        compiler_params=_cp(2), name="hy_conv")(t_re, t_im, k_re, k_im, tabs["f_blk"], tabs["fi_blk"])


def _idft_a_kernel(cr_ref, ci_ref, g_ref, x_ref, z_ref, d_ref, o_ref, ob_ref, *, n_sub, cols):
    for s in range(n_sub):
        sl = slice(s * cols, (s + 1) * cols)
        c = jnp.concatenate([cr_ref[:, sl], ci_ref[:, sl]], axis=0)
        y = jnp.dot(g_ref[s], c, preferred_element_type=F32)
        out = x_ref[:, sl].astype(F32) * (y + z_ref[:, sl] * d_ref[...])
        o_ref[:, sl] = out
        ob_ref[:, sl] = out.astype(ob_ref.dtype)


def _idft_a(c_re, c_im, tabs, x_mul, z_prev, d_term, *, seq, cols):
    h1, k1p = tabs["h1"], tabs["k1p"]
    n_sub = max(1, min(DFT_N2, 8192 // cols))
    wide = n_sub * cols
    spec_c = pl.BlockSpec((k1p, wide), lambda i: (0, i))
    spec_x = pl.BlockSpec((h1, wide), lambda i: (0, i))
    view = lambda a: a.reshape(h1, DFT_N2 * cols)
    o, ob = pl.pallas_call(
        functools.partial(_idft_a_kernel, n_sub=n_sub, cols=cols), grid=(DFT_N2 // n_sub,),
        in_specs=[spec_c, spec_c, pl.BlockSpec((n_sub, h1, 2 * k1p), lambda i: (i, 0, 0)), spec_x, spec_x,
                  pl.BlockSpec((1, cols), lambda i: (0, 0))],
        out_specs=[spec_x, spec_x],
        out_shape=[jax.ShapeDtypeStruct((h1, DFT_N2 * cols), F32),
                   jax.ShapeDtypeStruct((h1, DFT_N2 * cols), BF16)],
        compiler_params=_cp(1), name="idft_outer")(
            c_re.reshape(k1p, DFT_N2 * cols), c_im.reshape(k1p, DFT_N2 * cols), tabs["inv"],
            view(x_mul), view(z_prev), d_term.reshape(1, cols))
    return o.reshape(seq, cols), ob.reshape(seq, cols)


def _hyena(hy_u, conv_w, conv_b, filt_params, d_term, feats, deltas, tabs, *, seq, width):
    hwin, l1 = _hy_filters(feats, deltas, *filt_params, seq=seq, width=width, h1=tabs["h1"])
    n_cols = HY_ORDER * width
    l1 = l1[0, :n_cols] + l1[0, n_cols:]
    wnorm = (1.0 / (tabs["n"] * (l1 + EPS))).reshape(1, n_cols)
    h_re, h_im = _dft_a(hwin, tabs, cols=2 * n_cols)
    k_re, k_im = _hy_filter_spectrum(h_re, h_im, tabs, wnorm, cols=n_cols)
    x0, x1, v, v_bf = _hy_short_conv(hy_u, conv_w, conv_b, seq=seq, width=width)
    z, z_bf = v, v_bf
    for order, x_mul in enumerate((x0, x1)):
        t_re, t_im = _dft_a(z_bf, tabs, cols=width)
        c_re, c_im = _hy_conv(t_re, t_im, k_re, k_im, tabs, order=order, cols=width)
        z, z_bf = _idft_a(c_re, c_im, tabs, x_mul, z, d_term[order], seq=seq, cols=width)
    return z_bf


def _merge_kernel(gl_ref, oa_ref, ob_ref, oc_ref, od_ref, wg_ref, wb_ref, bg_ref, o_ref):
    gl = gl_ref[...]
    acc = None
    for i, o in enumerate((oa_ref, ob_ref, oc_ref, od_ref)):
        gate = jax.nn.sigmoid(jnp.dot(gl, wg_ref[i], preferred_element_type=F32) + bg_ref[i])
        term = gate * jnp.dot(o[...], wb_ref[i], preferred_element_type=F32)
        acc = term if acc is None else acc + term
    o_ref[...] = acc.astype(o_ref.dtype)


def _merge(proj, branches, w_gate, b_gate, w_branch, layer, *, gate_blk, tm=512, tn=1024):
    m, unit = branches[0].shape
    _, nbr, rank, n = w_gate.shape
    tm, tn = _tile(m, tm), _tile(n, tn)
    br = pl.BlockSpec((tm, unit), lambda j, i: (i, 0))
    return pl.pallas_call(
        _merge_kernel, grid=(n // tn, m // tm),
        in_specs=[pl.BlockSpec((tm, rank), lambda j, i: (i, gate_blk)), br, br, br, br,
                  pl.BlockSpec((None, nbr, rank, tn), lambda j, i: (layer, 0, 0, j)),
                  pl.BlockSpec((None, nbr, unit, tn), lambda j, i: (layer, 0, 0, j)),
                  pl.BlockSpec((nbr, 1, tn), lambda j, i: (0, 0, j))],
        out_specs=pl.BlockSpec((tm, tn), lambda j, i: (i, j)),
        out_shape=jax.ShapeDtypeStruct((m, n), BF16),
        compiler_params=_cp(2), name="merge")(proj, *branches, w_gate, w_branch, b_gate.reshape(nbr, 1, n))


def _xattn_kernel(h_ref, x_ref, wq_ref, kv_ref, wo_ref, gp_ref, gn_ref, xo_ref, ho_ref, *, heads, dh, qscale):
    q = (jnp.dot(h_ref[...], wq_ref[...], preferred_element_type=F32) * qscale).astype(BF16)
    kv = kv_ref[...]
    outs = []
    for hh in range(heads):
        k = kv[:, hh * dh:(hh + 1) * dh]
        v = kv[:, (heads + hh) * dh:(heads + hh + 1) * dh]
        s = lax.dot_general(q[:, hh * dh:(hh + 1) * dh], k, (((1,), (1,)), ((), ())),
                            preferred_element_type=F32)
        p = jnp.exp2(s - jnp.max(s, axis=-1, keepdims=True))
        o = jnp.dot(p.astype(BF16), v, preferred_element_type=F32) / jnp.sum(p, axis=-1, keepdims=True)
        outs.append(o.astype(BF16))
    y = jnp.dot(jnp.concatenate(outs, axis=1), wo_ref[...], preferred_element_type=F32)
    xn = x_ref[...] + _rms(y, gp_ref[...])
    xo_ref[...] = xn
    ho_ref[...] = _rms(xn, gn_ref[...]).astype(ho_ref.dtype)


def _cross_attention(h, x, kv, wq, wo, layer, g_post, g_next):
    m, d = x.shape
    dh, heads = HEAD_DIM, XA_HEADS
    n_mem = kv.shape[0]
    tm = _tile(m, 256)
    row = pl.BlockSpec((tm, d), lambda i: (i, 0))
    vec = pl.BlockSpec((1, d), lambda i: (0, 0))
    full = lambda shape: pl.BlockSpec(shape, lambda i: (0, 0))
    kern = functools.partial(_xattn_kernel, heads=heads, dh=dh, qscale=dh ** -0.5 * LOG2E)
    return pl.pallas_call(
        kern, grid=(m // tm,),
        in_specs=[row, row, pl.BlockSpec((None, d, heads * dh), lambda i: (layer, 0, 0)),
                  full((n_mem, 2 * heads * dh)), pl.BlockSpec((None, heads * dh, d), lambda i: (layer, 0, 0)), vec, vec],
        out_specs=[row, row],
        out_shape=[jax.ShapeDtypeStruct((m, d), F32), jax.ShapeDtypeStruct((m, d), BF16)],
        compiler_params=_cp(1), name="cross_attention")(
            h, x, wq, kv, wo, g_post.reshape(1, d), g_next.reshape(1, d))


def _rope_tables(seq):
    dh = HEAD_DIM
    n_rows = seq // GRID_W
    row = jnp.broadcast_to(jnp.arange(n_rows, dtype=F32)[:, None], (n_rows, GRID_W)).reshape(seq)
    col = jnp.broadcast_to(jnp.arange(GRID_W, dtype=F32)[None, :], (n_rows, GRID_W)).reshape(seq)
    axis_dim = dh // 2
    inv_freq = ROPE_THETA ** (-jnp.arange(0, axis_dim, 2, dtype=F32) / axis_dim)
    ang = jnp.stack([row[:, None] * inv_freq, col[:, None] * inv_freq], axis=1)
    ang = jnp.broadcast_to(ang[:, :, None, :], (seq, 2, 2, axis_dim // 2)).reshape(seq, dh)
    sign = jnp.where((jnp.arange(dh) % (dh // 2)) < dh // 4, -1.0, 1.0).astype(F32)
    return jnp.cos(ang), jnp.sin(ang) * sign[None, :]


def _hyena_tables(seq, width):
    bands = (HY_EMB - 1) // 2
    pos = jnp.arange(seq, dtype=F32)
    t = pos / (seq - 1)
    w = 2.0 * math.pi * pos / seq
    f = jnp.linspace(1e-4, bands - 1, bands, dtype=F32)
    ang = w[:, None] * f[None]
    feats = jnp.concatenate([t[:, None], jnp.cos(ang), -jnp.sin(ang)], axis=-1)
    deltas = jnp.abs(jnp.linspace(math.log(HY_DECAY_TARGET) / HY_SLOW_PCT,
                                  math.log(HY_DECAY_TARGET) / HY_FAST_PCT, width, dtype=F32))
    return feats, deltas


def kernel(x, mem, ffn1_pre_norm, ffn1_w1, ffn1_w3, ffn1_w2, ffn1_post_norm, mix_pre_norm, w_in, diff_lambda, diff_norm, gqa_q_norm, gqa_k_norm, ret_decay_logit, ret_norm, hy_conv_w, hy_conv_b, hy_w1, hy_b1, hy_w2, hy_b2, hy_w3, hy_sin_freq, hy_filter_bias, w_gate_up, b_gate, w_branch, w_out, mix_post_norm, xa_pre_norm, xa_mem_norm, xa_wq, xa_wkv, xa_wo, xa_post_norm, ffn2_pre_norm, ffn2_w1, ffn2_w3, ffn2_w2, ffn2_post_norm):
    batch, seq, d_model = x.shape
    assert batch == 1
    depth = w_in.shape[0]
    dh = HEAD_DIM
    unit = d_model // 4
    da_heads = unit // (2 * dh)
    gqa_heads = unit // dh
    gqa_kv = gqa_heads // 4
    ret_heads = unit // (2 * dh)
    hy_w = HY_ORDER + 1
    names = ("qa", "ka", "va", "qb", "kb", "vb", "qc", "kc", "vc", "gc", "hy", "gate")
    widths = (da_heads * 2 * dh, da_heads * 2 * dh, da_heads * 2 * dh, gqa_heads * dh, gqa_kv * dh,
              gqa_kv * dh, ret_heads * dh, ret_heads * dh, ret_heads * 2 * dh, ret_heads * 2 * dh,
              hy_w * unit, GATE_RANK)
    assert sum(widths) == w_in.shape[2]
    src = dict(zip(names, np.concatenate([[0], np.cumsum(widths)[:-1]]).tolist()))
    wid = dict(zip(names, widths))
    blk_w = dict(qa=2 * dh, ka=2 * dh, va=2 * dh, qb=wid["qb"], kb=wid["kb"], vb=wid["vb"], qc=dh, kc=dh,
                 vc=2 * dh, gc=2 * dh, gate=GATE_RANK)
    order_a = sorted(blk_w, key=lambda nm: -blk_w[nm])
    off, pos = {}, 0
    for nm in order_a:
        assert pos % blk_w[nm] == 0 and wid[nm] % blk_w[nm] == 0, nm
        off[nm] = pos
        pos += wid[nm]
    n_a = pos
    assert off["ka"] == off["qa"] + wid["qa"]

    att_scale = dh ** -0.5 * LOG2E
    colscale = jnp.ones((n_a,), F32)
    colscale = colscale.at[off["qa"]:off["qa"] + wid["qa"]].set(att_scale)
    colscale = colscale.at[off["kc"]:off["kc"] + wid["kc"]].set(dh ** -0.5)

    slopes = 2.0 ** (-8.0 * jnp.arange(1, da_heads + 1, dtype=F32) / da_heads)
    rope_cos, rope_sin = _rope_tables(seq)
    hy_feats, hy_deltas = _hyena_tables(seq, unit)
    tabs = _dft_tables(seq)

    ffn1_w = tuple(w.astype(BF16) for w in (ffn1_w1, ffn1_w3, ffn1_w2))
    ffn2_w = tuple(w.astype(BF16) for w in (ffn2_w1, ffn2_w3, ffn2_w2))
    w_gate_b, w_branch_b, w_out_b = w_gate_up.astype(BF16), w_branch.astype(BF16), w_out.astype(BF16)
    xa_wq_b, xa_wkv_b, xa_wo_b = xa_wq.astype(BF16), xa_wkv.astype(BF16), xa_wo.astype(BF16)

    xs = x.reshape(seq, d_model)
    mem2 = mem.reshape(mem.shape[1], d_model)
    h = _norm_cast(xs, ffn1_pre_norm[0])
    for l in range(depth):
        xs, h = _ffn(xs, h, *ffn1_w, l, ffn1_post_norm[l], mix_pre_norm[l])

        w_l = w_in[l]
        w_a = jnp.concatenate([w_l[:, src[nm]:src[nm] + wid[nm]] for nm in order_a], axis=1).astype(BF16)
        proj = _matmul(h, w_a, BF16, colscale)
        hy_u = _matmul(h, w_l[:, src["hy"]:src["gate"]].astype(BF16), F32)

        lam_init = 0.8 - 0.6 * math.exp(-0.3 * l)
        lp = diff_lambda[l]
        lam = jnp.exp(jnp.sum(lp[0] * lp[1])) - jnp.exp(jnp.sum(lp[2] * lp[3])) + lam_init
        scal = jnp.concatenate([lam.reshape(1), slopes * LOG2E]).astype(F32)
        top = _head_norm_max(proj, seq=seq, first_blk=off["qa"] // (2 * dh), n_blk=2 * da_heads)
        spread = 2.0 * jnp.max(jnp.sqrt(top[:da_heads] * top[da_heads:]), axis=1)
        reach = jnp.ceil((ALIBI_ZERO_LOG2 + spread) / (slopes * LOG2E))
        reach = jnp.where(reach < seq, jnp.maximum(reach, 0.0), seq).astype(jnp.int32)
        da_tq, da_tk = _tile(seq, DIFF_QUERY_BLOCK), _tile(seq // FLASH_SUBSTEPS, DIFF_KEY_BLOCK)
        schedule = _alibi_schedule(reach, seq=seq, tq=da_tq, span=FLASH_SUBSTEPS * da_tk)
        oa = _flash_attention(proj, proj, proj, seq=seq, heads=da_heads, n_maps=2, q_blk=off["qa"] // (2 * dh),
                              k_blk=off["ka"] // (2 * dh), v_blk=off["va"] // (2 * dh), tq=da_tq, tk=da_tk,
                              diff_args=(scal, diff_norm[l], 1.0 - lam_init, schedule))

        qg, kg, vg = _gqa_prep(proj, rope_cos, rope_sin, gqa_q_norm[l], gqa_k_norm[l], seq=seq,
                               q_w=wid["qb"], kv_w=wid["kb"], q_blk=off["qb"] // wid["qb"],
                               k_blk=off["kb"] // wid["kb"], v_blk=off["vb"] // wid["vb"], qscale=att_scale)
        ob = _flash_attention(qg, kg, vg, seq=seq, heads=gqa_kv, n_maps=gqa_heads // gqa_kv, q_blk=0, k_blk=0,
                              v_blk=0, tq=256, tk=GQA_KEY_BLOCK)

        log_g = -jax.nn.softplus(-ret_decay_logit[l].astype(F32))
        ret_args = dict(seq=seq, heads=ret_heads, q_blk=off["qc"] // dh, k_blk=off["kc"] // dh,
                        v_blk=off["vc"] // (2 * dh), gate_blk=off["gc"] // (2 * dh))
        y_bwd = _retention_pass(proj, _retention_tables(log_g[1], True), reverse=True, **ret_args)
        oc = _retention_pass(proj, _retention_tables(log_g[0], False), reverse=False, y_bwd=y_bwd,
                             gain=ret_norm[l], **ret_args)

        od = _hyena(hy_u, hy_conv_w[l], hy_conv_b[l],
                    (hy_w1[l], hy_b1[l], hy_w2[l], hy_b2[l], hy_w3[l], hy_sin_freq[l]),
                    hy_filter_bias[l], hy_feats, hy_deltas, tabs, seq=seq, width=unit)

        merged = _merge(proj, (oa, ob, oc, od), w_gate_b, b_gate[l], w_branch_b, l,
                        gate_blk=off["gate"] // GATE_RANK)
        y = _matmul(merged, w_out_b, BF16, layer=l)
        xs, h = _resid_norm(xs, y, mix_post_norm[l], 1.0, xa_pre_norm[l])

        mem_n = _norm_cast(mem2, xa_mem_norm[l])
        kv = _matmul(mem_n, xa_wkv_b, BF16, layer=l)
        xs, h = _cross_attention(h, xs, kv, xa_wq_b, xa_wo_b, l, xa_post_norm[l], ffn2_pre_norm[l])

        g_next = ffn1_pre_norm[l + 1] if l + 1 < depth else ffn2_pre_norm[l]
        xs, h = _ffn(xs, h, *ffn2_w, l, ffn2_post_norm[l], g_next)
    return xs.reshape(batch, seq, d_model)
```
